```python
import math
import jax, jax.numpy as jnp
from jax import lax
import numpy as np

D_MODEL = 2048
BATCH = 2
SEQ = 4096
DEPTH = 2
DEC_BATCH = 32
DEC_SEQ = 4
PAST_LEN = 8192
PAGE_SIZE = 128

N_MOBA_LAYERS = (DEPTH + 1) // 2
N_DIFF_LAYERS = DEPTH // 2
MOBA_HEADS = 16
MOBA_HEAD_DIM = D_MODEL // MOBA_HEADS
MOBA_BLOCK = 256
MOBA_TOPK = 3
MOBA_Q_CHUNK = 16
MOBA_SCALE = MOBA_HEAD_DIM ** -0.5
DIFF_HEADS = 8
DIFF_HEAD_DIM = D_MODEL // (2 * DIFF_HEADS)
DIFF_Q_BLOCK = 128
DIFF_SCALE = DIFF_HEAD_DIM ** -0.5
N_EXPERTS = 64
TOP_K = 8
N_GROUPS = 8
TOPK_GROUPS = 4
EXPERTS_PER_GROUP = N_EXPERTS // N_GROUPS
D_EXPERT = 512
D_SHARED = 512
ROUTED_SCALE = 2.5
MOE_BLOCK = 64
RMS_EPS = 1e-6
NEG_INF = -1e30

kernel_name = 'moba_diffattn_moe_adaln_hybrid_step'


def rms_norm(x, g):
    xf = x.astype(jnp.float32)
    y = xf * lax.rsqrt(jnp.mean(xf * xf, axis=-1, keepdims=True) + RMS_EPS)
    return (y * g.astype(jnp.float32)).astype(x.dtype)


def alibi_slopes(n_heads):
    return jnp.asarray([2.0 ** (-8.0 * (h + 1) / n_heads) for h in range(n_heads)], dtype=jnp.float32)


def ada_modulation(c, w, b):
    m = jnp.einsum('nd,de->ne', jax.nn.silu(c), w) + b
    shift, scale, gate = jnp.split(m[:, None, :], 3, axis=-1)
    return shift, scale, gate


def modulated_norm(x, g, shift, scale):
    return rms_norm(x, g) * (1.0 + scale) + shift


def gather_pages(cache, layer, pages):
    rows = cache[layer, pages]
    return rows.reshape((-1,) + rows.shape[2:])


def moba_qkv(h, w_qkv, g_q, g_k):
    n, t, _ = h.shape
    qkv = jnp.einsum('ntd,de->nte', h, w_qkv).reshape(n, t, 3, MOBA_HEADS, MOBA_HEAD_DIM)
    return rms_norm(qkv[:, :, 0], g_q), rms_norm(qkv[:, :, 1], g_k), qkv[:, :, 2]


def to_blocks(k, v):
    n, length, h, d = k.shape
    nb = -(-length // MOBA_BLOCK)
    pad = ((0, 0), (0, nb * MOBA_BLOCK - length), (0, 0), (0, 0))
    kb = jnp.pad(k, pad).reshape(n, nb, MOBA_BLOCK, h, d).transpose(0, 3, 1, 2, 4)
    vb = jnp.pad(v, pad).reshape(n, nb, MOBA_BLOCK, h, d).transpose(0, 3, 1, 2, 4)
    k_mean = jnp.mean(kb.astype(jnp.float32), axis=3)
    return kb, vb, k_mean


def moba_attend(q, q_pos, k_blocks, v_blocks, k_mean, slopes):
    n, tq, h, d = q.shape
    nb = k_blocks.shape[2]
    n_sel = min(MOBA_TOPK, nb)
    qf = q.astype(jnp.float32)
    own = q_pos // MOBA_BLOCK
    score = jnp.einsum('nqhd,nhbd->nhqb', qf, k_mean)
    fully_past = jnp.arange(nb, dtype=jnp.int32)[None, :] < own[:, None]
    score = jnp.where(fully_past, score, NEG_INF)
    _, sel = lax.top_k(score, n_sel)
    own_b = jnp.broadcast_to(own[None, None, :, None], (n, h, tq, 1)).astype(sel.dtype)
    blocks = jnp.concatenate([sel, own_b], axis=-1)
    blk_ok = jnp.concatenate([sel < own[None, None, :, None], jnp.ones(own_b.shape, dtype=bool)], axis=-1)
    n_idx = jnp.arange(n)[:, None, None, None]
    h_idx = jnp.arange(h)[None, :, None, None]
    kg = k_blocks[n_idx, h_idx, blocks]
    vg = v_blocks[n_idx, h_idx, blocks]
    key_pos = blocks[..., None] * MOBA_BLOCK + jnp.arange(MOBA_BLOCK, dtype=jnp.int32)
    dist = q_pos[None, None, :, None, None] - key_pos
    logits = jnp.einsum('nqhd,nhqkjd->nhqkj', qf, kg.astype(jnp.float32)) * MOBA_SCALE
    logits = logits - slopes[None, :, None, None, None] * dist.astype(jnp.float32)
    logits = jnp.where(blk_ok[..., None] & (dist >= 0), logits, NEG_INF)
    p = jax.nn.softmax(logits.reshape(n, h, tq, -1), axis=-1).reshape(logits.shape)
    out = jnp.einsum('nhqkj,nhqkjd->nqhd', p, vg.astype(jnp.float32))
    return out.astype(q.dtype)


def moba_prompt(q, k, v, slopes):
    n, s, h, d = q.shape
    kb, vb, km = to_blocks(k, v)
    nc = s // MOBA_Q_CHUNK
    qc = q.reshape(n, nc, MOBA_Q_CHUNK, h, d).swapaxes(0, 1)
    pos = jnp.arange(s, dtype=jnp.int32).reshape(nc, MOBA_Q_CHUNK)
    out = lax.map(lambda a: moba_attend(a[0], a[1], kb, vb, km, slopes), (qc, pos))
    return out.swapaxes(0, 1).reshape(n, s, h * d)


def moba_sample(q, k, v, cache_k, cache_v, layer, page_table, slopes):
    bd, tn, h, d = q.shape

    def one(a):
        q1, k1, v1, pt = a
        kp = gather_pages(cache_k, layer, pt)
        vp = gather_pages(cache_v, layer, pt)
        k_all = jnp.concatenate([kp, k1.astype(kp.dtype)], axis=0)[None]
        v_all = jnp.concatenate([vp, v1.astype(vp.dtype)], axis=0)[None]
        kb, vb, km = to_blocks(k_all, v_all)
        q_pos = kp.shape[0] + jnp.arange(tn, dtype=jnp.int32)
        return moba_attend(q1[None], q_pos, kb, vb, km, slopes)[0]

    out = lax.map(one, (q, k, v, page_table))
    return out.reshape(bd, tn, h * d)


def diff_qkv(h, w_qkv, g_q, g_k):
    n, t, _ = h.shape
    qkv = jnp.einsum('ntd,de->nte', h, w_qkv)
    q = qkv[..., :D_MODEL].reshape(n, t, DIFF_HEADS, 2, DIFF_HEAD_DIM)
    k = qkv[..., D_MODEL:2 * D_MODEL].reshape(n, t, DIFF_HEADS, 2, DIFF_HEAD_DIM)
    v = qkv[..., 2 * D_MODEL:].reshape(n, t, DIFF_HEADS, 2 * DIFF_HEAD_DIM)
    return rms_norm(q, g_q), rms_norm(k, g_k), v


def diff_attend(q, q_pos, k, v, lam, slopes):
    k_pos = jnp.arange(k.shape[1], dtype=jnp.int32)
    dist = q_pos[:, None] - k_pos[None, :]
    logits = jnp.einsum('nqhcd,nkhcd->nhcqk', q.astype(jnp.float32), k.astype(jnp.float32)) * DIFF_SCALE
    logits = logits - slopes[None, :, None, None, None] * dist.astype(jnp.float32)
    logits = jnp.where(dist >= 0, logits, NEG_INF)
    p = jax.nn.softmax(logits, axis=-1)
    p = p[:, :, 0] - lam * p[:, :, 1]
    return jnp.einsum('nhqk,nkhe->nqhe', p, v.astype(jnp.float32))


def diff_prompt(q, k, v, lam, slopes):
    n, s = q.shape[:2]
    nq = s // DIFF_Q_BLOCK
    qb = q.reshape(n, nq, DIFF_Q_BLOCK, DIFF_HEADS, 2, DIFF_HEAD_DIM).swapaxes(0, 1)
    pos = jnp.arange(s, dtype=jnp.int32).reshape(nq, DIFF_Q_BLOCK)
    out = lax.map(lambda a: diff_attend(a[0], a[1], k, v, lam, slopes), (qb, pos))
    return out.swapaxes(0, 1).reshape(n, s, DIFF_HEADS, 2 * DIFF_HEAD_DIM)


def diff_sample(q, k, v, cache_k, cache_v, layer, page_table, lam, slopes):
    def one(a):
        q1, k1, v1, pt = a
        kp = gather_pages(cache_k, layer, pt)
        vp = gather_pages(cache_v, layer, pt)
        k_all = jnp.concatenate([kp, k1.astype(kp.dtype)], axis=0)[None]
        v_all = jnp.concatenate([vp, v1.astype(vp.dtype)], axis=0)[None]
        q_pos = kp.shape[0] + jnp.arange(q1.shape[0], dtype=jnp.int32)
        return diff_attend(q1[None], q_pos, k_all, v_all, lam, slopes)[0]

    return lax.map(one, (q, k, v, page_table))


def diff_output(o, g_sub, lam_init, w_o, dtype):
    n, t = o.shape[:2]
    o = (rms_norm(o, g_sub) * (1.0 - lam_init)).astype(dtype).reshape(n, t, D_MODEL)
    return jnp.einsum('ntd,de->nte', o, w_o)


def swiglu(x, wg, wu, wd):
    return jnp.dot(jax.nn.silu(jnp.dot(x, wg)) * jnp.dot(x, wu), wd)


def route(h, w_r, b_r):
    t = h.shape[0]
    s = jax.nn.sigmoid(jnp.dot(h.astype(jnp.float32), w_r.astype(jnp.float32)))
    choice = s + b_r.astype(jnp.float32)
    grp_score = lax.top_k(choice.reshape(t, N_GROUPS, EXPERTS_PER_GROUP), 2)[0].sum(-1)
    _, grp_idx = lax.top_k(grp_score, TOPK_GROUPS)
    grp_mask = jnp.any(grp_idx[:, :, None] == jnp.arange(N_GROUPS)[None, None, :], axis=1)
    choice = jnp.where(jnp.repeat(grp_mask, EXPERTS_PER_GROUP, axis=1), choice, NEG_INF)
    _, idx = lax.top_k(choice, TOP_K)
    w = jnp.take_along_axis(s, idx, axis=1)
    w = w / jnp.sum(w, axis=-1, keepdims=True) * ROUTED_SCALE
    return idx, w


def routed_experts(h, idx, gw, wg_e, wu_e, wd_e):
    t, d = h.shape
    a = t * TOP_K
    flat_e = idx.reshape(a)
    order = jnp.argsort(flat_e)
    e_sorted = flat_e[order]
    tok_sorted = order // TOP_K
    counts = jnp.bincount(flat_e, length=N_EXPERTS)
    padded = (counts + MOE_BLOCK - 1) // MOE_BLOCK * MOE_BLOCK
    pad_end = jnp.cumsum(padded)
    pad_start = pad_end - padded
    start = jnp.cumsum(counts) - counts
    dest = pad_start[e_sorted] + jnp.arange(a) - start[e_sorted]
    n_blocks = -(-a // MOE_BLOCK) + N_EXPERTS
    slot_tok = jnp.zeros((n_blocks * MOE_BLOCK,), jnp.int32).at[dest].set(tok_sorted.astype(jnp.int32))
    block_e = jnp.minimum(jnp.searchsorted(pad_end, jnp.arange(n_blocks) * MOE_BLOCK, side='right'), N_EXPERTS - 1)
    xb = h[slot_tok].reshape(n_blocks, MOE_BLOCK, d)

    def expert_block(args):
        xblk, e = args
        return swiglu(xblk, wg_e[e], wu_e[e], wd_e[e])

    yb = lax.map(expert_block, (xb, block_e)).reshape(n_blocks * MOE_BLOCK, d)
    y_sorted = yb[dest].astype(jnp.float32) * gw.reshape(a)[order][:, None]
    return jnp.zeros((t, d), jnp.float32).at[tok_sorted].add(y_sorted).astype(h.dtype)


def moe_ffn(h, w_r, b_r, wg_e, wu_e, wd_e, wg_s, wu_s, wd_s):
    n, t, d = h.shape
    hf = h.reshape(n * t, d)
    idx, gw = route(hf, w_r, b_r)
    y = routed_experts(hf, idx, gw, wg_e, wu_e, wd_e) + swiglu(hf, wg_s, wu_s, wd_s)
    return y.reshape(n, t, d)


def setup_inputs(seed: int = 0) -> dict:
    key = jax.random.key(seed)
    ks = iter(jax.random.split(key, 64))
    D = D_MODEL

    def nrm(shape, scale):
        return jax.random.normal(next(ks), shape, jnp.float32) * scale

    def gain(shape):
        return 1.0 + nrm(shape, 0.02)

    n_pages = PAST_LEN // PAGE_SIZE
    n_used = DEC_BATCH * n_pages
    n_pool = n_used + max(1, n_used // 4)
    page_table = jax.random.permutation(next(ks), n_pool)[:n_used].reshape(DEC_BATCH, n_pages).astype(jnp.int32)
    return {
        'x_prompt': nrm((BATCH, SEQ, D), 1.0),
        'x_sample': nrm((DEC_BATCH, DEC_SEQ, D), 1.0),
        'cache_k_moba': nrm((N_MOBA_LAYERS, n_pool, PAGE_SIZE, MOBA_HEADS, MOBA_HEAD_DIM), 1.0),
        'cache_v_moba': nrm((N_MOBA_LAYERS, n_pool, PAGE_SIZE, MOBA_HEADS, MOBA_HEAD_DIM), 1.0),
        'cache_k_diff': nrm((N_DIFF_LAYERS, n_pool, PAGE_SIZE, DIFF_HEADS, 2, DIFF_HEAD_DIM), 1.0),
        'cache_v_diff': nrm((N_DIFF_LAYERS, n_pool, PAGE_SIZE, DIFF_HEADS, 2 * DIFF_HEAD_DIM), 1.0),
        'page_table': page_table,
        'c_prompt': nrm((BATCH, D), 1.0),
        'c_sample': nrm((DEC_BATCH, D), 1.0),
        'g_mix': gain((DEPTH, D)),
        'w_ada_mix': nrm((DEPTH, D, 3 * D), 0.5 * D ** -0.5),
        'b_ada_mix': nrm((DEPTH, 3 * D), 0.01),
        'w_qkv_moba': nrm((N_MOBA_LAYERS, D, 3 * D), D ** -0.5),
        'g_q_moba': gain((N_MOBA_LAYERS, MOBA_HEAD_DIM)),
        'g_k_moba': gain((N_MOBA_LAYERS, MOBA_HEAD_DIM)),
        'w_o_moba': nrm((N_MOBA_LAYERS, D, D), D ** -0.5),
        'w_qkv_diff': nrm((N_DIFF_LAYERS, D, 3 * D), D ** -0.5),
        'g_q_diff': gain((N_DIFF_LAYERS, 2, DIFF_HEAD_DIM)),
        'g_k_diff': gain((N_DIFF_LAYERS, 2, DIFF_HEAD_DIM)),
        'lambda_q1': nrm((N_DIFF_LAYERS, DIFF_HEAD_DIM), 0.1),
        'lambda_k1': nrm((N_DIFF_LAYERS, DIFF_HEAD_DIM), 0.1),
        'lambda_q2': nrm((N_DIFF_LAYERS, DIFF_HEAD_DIM), 0.1),
        'lambda_k2': nrm((N_DIFF_LAYERS, DIFF_HEAD_DIM), 0.1),
        'g_sub_diff': gain((N_DIFF_LAYERS, 2 * DIFF_HEAD_DIM)),
        'w_o_diff': nrm((N_DIFF_LAYERS, D, D), D ** -0.5),
        'g_ffn': gain((DEPTH, D)),
        'w_ada_ffn': nrm((DEPTH, D, 3 * D), 0.5 * D ** -0.5),
        'b_ada_ffn': nrm((DEPTH, 3 * D), 0.01),
        'w_router': nrm((DEPTH, D, N_EXPERTS), D ** -0.5),
        'b_router': nrm((DEPTH, N_EXPERTS), 0.01),
        'w_gate_exp': nrm((DEPTH, N_EXPERTS, D, D_EXPERT), D ** -0.5),
        'w_up_exp': nrm((DEPTH, N_EXPERTS, D, D_EXPERT), D ** -0.5),
        'w_down_exp': nrm((DEPTH, N_EXPERTS, D_EXPERT, D), D_EXPERT ** -0.5),
        'w_gate_sh': nrm((DEPTH, D, D_SHARED), D ** -0.5),
        'w_up_sh': nrm((DEPTH, D, D_SHARED), D ** -0.5),
        'w_down_sh': nrm((DEPTH, D_SHARED, D), D_SHARED ** -0.5),
    }


def reference(x_prompt, x_sample, cache_k_moba, cache_v_moba, cache_k_diff, cache_v_diff, page_table,
              c_prompt, c_sample, g_mix, w_ada_mix, b_ada_mix, w_qkv_moba, g_q_moba, g_k_moba, w_o_moba,
              w_qkv_diff, g_q_diff, g_k_diff, lambda_q1, lambda_k1, lambda_q2, lambda_k2, g_sub_diff, w_o_diff,
              g_ffn, w_ada_ffn, b_ada_ffn, w_router, b_router, w_gate_exp, w_up_exp, w_down_exp,
              w_gate_sh, w_up_sh, w_down_sh):
    xp, xs = x_prompt, x_sample
    slopes_moba = alibi_slopes(MOBA_HEADS)
    slopes_diff = alibi_slopes(DIFF_HEADS)
    k_mp, v_mp, k_dp, v_dp = [], [], [], []
    k_ms, v_ms, k_ds, v_ds = [], [], [], []
    for i in range(DEPTH):
        j = i // 2
        shp, scp, gtp = ada_modulation(c_prompt, w_ada_mix[i], b_ada_mix[i])
        shs, scs, gts = ada_modulation(c_sample, w_ada_mix[i], b_ada_mix[i])
        hp = modulated_norm(xp, g_mix[i], shp, scp)
        hs = modulated_norm(xs, g_mix[i], shs, scs)
        if i % 2 == 0:
            qp, kp, vp = moba_qkv(hp, w_qkv_moba[j], g_q_moba[j], g_k_moba[j])
            qs, ks_, vs = moba_qkv(hs, w_qkv_moba[j], g_q_moba[j], g_k_moba[j])
            op = moba_prompt(qp, kp, vp, slopes_moba)
            os_ = moba_sample(qs, ks_, vs, cache_k_moba, cache_v_moba, j, page_table, slopes_moba)
            mix_p = jnp.einsum('ntd,de->nte', op, w_o_moba[j])
            mix_s = jnp.einsum('ntd,de->nte', os_, w_o_moba[j])
            k_mp.append(kp); v_mp.append(vp); k_ms.append(ks_); v_ms.append(vs)
        else:
            lam_init = 0.8 - 0.6 * math.exp(-0.3 * i)
            lam = (jnp.exp(jnp.sum(lambda_q1[j].astype(jnp.float32) * lambda_k1[j].astype(jnp.float32)))
                   - jnp.exp(jnp.sum(lambda_q2[j].astype(jnp.float32) * lambda_k2[j].astype(jnp.float32)))
                   + lam_init)
            qp, kp, vp = diff_qkv(hp, w_qkv_diff[j], g_q_diff[j], g_k_diff[j])
            qs, ks_, vs = diff_qkv(hs, w_qkv_diff[j], g_q_diff[j], g_k_diff[j])
            op = diff_prompt(qp, kp, vp, lam, slopes_diff)
            os_ = diff_sample(qs, ks_, vs, cache_k_diff, cache_v_diff, j, page_table, lam, slopes_diff)
            mix_p = diff_output(op, g_sub_diff[j], lam_init, w_o_diff[j], xp.dtype)
            mix_s = diff_output(os_, g_sub_diff[j], lam_init, w_o_diff[j], xs.dtype)
            k_dp.append(kp); v_dp.append(vp); k_ds.append(ks_); v_ds.append(vs)
        xp = xp + gtp * mix_p
        xs = xs + gts * mix_s
        shp, scp, gtp = ada_modulation(c_prompt, w_ada_ffn[i], b_ada_ffn[i])
        shs, scs, gts = ada_modulation(c_sample, w_ada_ffn[i], b_ada_ffn[i])
        wge, wue, wde = w_gate_exp[i], w_up_exp[i], w_down_exp[i]
        hp = modulated_norm(xp, g_ffn[i], shp, scp)
        hs = modulated_norm(xs, g_ffn[i], shs, scs)
        xp = xp + gtp * moe_ffn(hp, w_router[i], b_router[i], wge, wue, wde, w_gate_sh[i], w_up_sh[i], w_down_sh[i])
        xs = xs + gts * moe_ffn(hs, w_router[i], b_router[i], wge, wue, wde, w_gate_sh[i], w_up_sh[i], w_down_sh[i])
    return (xp, xs, jnp.stack(k_mp), jnp.stack(v_mp), jnp.stack(k_dp), jnp.stack(v_dp),
            jnp.stack(k_ms), jnp.stack(v_ms), jnp.stack(k_ds), jnp.stack(v_ds))
```

```python
import functools
import math

import jax
import jax.numpy as jnp
from jax import lax
from jax.experimental import pallas as pl
from jax.experimental.pallas import tpu as pltpu

F32 = jnp.float32
BF16 = jnp.bfloat16
I32 = jnp.int32

MOBA_BLOCK = 256
MOBA_TOPK = 3
N_GROUPS = 8
TOPK_GROUPS = 4
TOP_K = 8
ROUTED_SCALE = 2.5
RMS_EPS = 1e-6
NEG_INF = -1e30

LANES = 128
VMEM_LIMIT = 48 * 1024 * 1024
NEW_PAD = 16

_NT = (((1,), (1,)), ((), ()))


def _params(*sem):
    return pltpu.CompilerParams(dimension_semantics=sem, vmem_limit_bytes=VMEM_LIMIT)


def _alibi_slopes(n_heads):
    return jnp.asarray([2.0 ** (-8.0 * (h + 1) / n_heads) for h in range(n_heads)], dtype=F32)


class _Mod:
    def __init__(self, arr, rows_per_seq):
        self.arr = arr
        self.rows_per_seq = rows_per_seq
        self.per_token = arr.ndim == 2

    def spec(self, part, tm, d, tn, row_axis, col_axis=None):
        ncol = d // tn

        def col(g):
            return part * ncol + (g[col_axis] if col_axis is not None else 0)

        if self.per_token:
            return pl.BlockSpec((tm, tn), lambda *g: (g[row_axis], col(g)))
        tps = self.rows_per_seq // tm
        return pl.BlockSpec((None, 1, tn), lambda *g: (g[row_axis] // tps, 0, col(g)))


def _ada_kernel(c_ref, w_ref, b_ref, o_ref):
    c = c_ref[...]
    a = (c * jax.nn.sigmoid(c)).astype(BF16)
    o_ref[...] = jnp.dot(a, w_ref[...].astype(BF16), preferred_element_type=F32) + b_ref[...]


def _ada(c_all, w, b):
    depth, d, n3 = w.shape
    mp = c_all.shape[0]
    tn = 512
    return pl.pallas_call(
        _ada_kernel,
        out_shape=jax.ShapeDtypeStruct((depth, mp, n3), F32),
        grid=(depth, n3 // tn),
        in_specs=[pl.BlockSpec((mp, d), lambda l, j: (0, 0)),
                  pl.BlockSpec((None, d, tn), lambda l, j: (l, 0, j)),
                  pl.BlockSpec((None, 1, tn), lambda l, j: (l, 0, j))],
        out_specs=pl.BlockSpec((None, mp, tn), lambda l, j: (l, 0, j)),
        compiler_params=_params("arbitrary", "arbitrary"),
        name="ada",
    )(c_all, w, b.reshape(depth, 1, n3))


def _modnorm_value(x, g, shift, scale):
    y = x * lax.rsqrt(jnp.mean(x * x, axis=-1, keepdims=True) + RMS_EPS)
    return (y * g) * (1.0 + scale) + shift


def _modnorm_kernel(x_ref, g_ref, sh_ref, sc_ref, h_ref):
    h_ref[...] = _modnorm_value(x_ref[...], g_ref[...], sh_ref[...], sc_ref[...]).astype(h_ref.dtype)


def _modnorm(x, g, mod, tm):
    t, d = x.shape
    return pl.pallas_call(
        _modnorm_kernel,
        out_shape=jax.ShapeDtypeStruct((t, d), BF16),
        grid=(t // tm,),
        in_specs=[pl.BlockSpec((tm, d), lambda i: (i, 0)),
                  pl.BlockSpec((1, d), lambda i: (0, 0)),
                  mod.spec(0, tm, d, d, 0),
                  mod.spec(1, tm, d, d, 0)],
        out_specs=pl.BlockSpec((tm, d), lambda i: (i, 0)),
        compiler_params=_params("arbitrary"),
        name="modnorm",
    )(x, g.reshape(1, d), mod.arr, mod.arr)


def _group_norm_store(y, g_ref, o_ref):
    for c in range(y.shape[1] // LANES):
        sl = slice(c * LANES, (c + 1) * LANES)
        yc = y[:, sl]
        yc = yc * lax.rsqrt(jnp.mean(yc * yc, axis=-1, keepdims=True) + RMS_EPS) * g_ref[:, sl]
        o_ref[:, sl] = yc.astype(o_ref.dtype)


def _qkv_kernel(h_ref, wq_ref, wk_ref, wv_ref, gq_ref, gk_ref, q_ref, k_ref, v_ref, wq_s, wk_s, wv_s):
    @pl.when(pl.program_id(1) == 0)
    def _():
        wq_s[...] = wq_ref[...].astype(BF16)
        wk_s[...] = wk_ref[...].astype(BF16)
        wv_s[...] = wv_ref[...].astype(BF16)

    h = h_ref[...]
    _group_norm_store(jnp.dot(h, wq_s[...], preferred_element_type=F32), gq_ref, q_ref)
    _group_norm_store(jnp.dot(h, wk_s[...], preferred_element_type=F32), gk_ref, k_ref)
    v_ref[...] = jnp.dot(h, wv_s[...], preferred_element_type=F32)


def _qkv(h, w, layer, gq, gk, tm):
    t, d = h.shape
    tn = 256
    nj = d // tn
    wspec = lambda part: pl.BlockSpec((None, d, tn), lambda j, i: (layer, 0, part * nj + j))
    gspec = pl.BlockSpec((1, tn), lambda j, i: (0, j))
    ospec = pl.BlockSpec((tm, tn), lambda j, i: (i, j))
    return pl.pallas_call(
        _qkv_kernel,
        out_shape=(jax.ShapeDtypeStruct((t, d), BF16), jax.ShapeDtypeStruct((t, d), F32),
                   jax.ShapeDtypeStruct((t, d), F32)),
        grid=(nj, t // tm),
        in_specs=[pl.BlockSpec((tm, d), lambda j, i: (i, 0)), wspec(0), wspec(1), wspec(2), gspec, gspec],
        out_specs=(ospec, ospec, ospec),
        scratch_shapes=[pltpu.VMEM((d, tn), BF16)] * 3,
        compiler_params=_params("arbitrary", "arbitrary"),
        name="qkv",
    )(h, w, w, w, gq.reshape(1, d), gk.reshape(1, d))


def _oproj_kernel(o_ref, w_ref, x_ref, gate_ref, out_ref, w_s):
    @pl.when(pl.program_id(1) == 0)
    def _():
        w_s[...] = w_ref[...].astype(BF16)

    out_ref[...] = x_ref[...] + gate_ref[...] * jnp.dot(o_ref[...], w_s[...], preferred_element_type=F32)


def _oproj(o, w, layer, x, mod, tm):
    t, d = o.shape
    tn = 512
    return pl.pallas_call(
        _oproj_kernel,
        out_shape=jax.ShapeDtypeStruct((t, d), F32),
        grid=(d // tn, t // tm),
        in_specs=[pl.BlockSpec((tm, d), lambda j, i: (i, 0)),
                  pl.BlockSpec((None, d, tn), lambda j, i: (layer, 0, j)),
                  pl.BlockSpec((tm, tn), lambda j, i: (i, j)),
                  mod.spec(2, tm, d, tn, 1, 0)],
        out_specs=pl.BlockSpec((tm, tn), lambda j, i: (i, j)),
        scratch_shapes=[pltpu.VMEM((d, tn), BF16)],
        compiler_params=_params("arbitrary", "arbitrary"),
        name="oproj",
    )(o, w, x, mod.arr)


def _top_lanes(score, n_pick):
    lane = lax.broadcasted_iota(I32, score.shape, 1).astype(F32)
    sel = jnp.zeros(score.shape, F32)
    for _ in range(n_pick):
        m = jnp.max(score, axis=-1, keepdims=True)
        first = jnp.min(jnp.where(score == m, lane, float(LANES)), axis=-1, keepdims=True)
        hit = lane == first
        sel = jnp.where(hit & (m > 0.5 * NEG_INF), 1.0, sel)
        score = jnp.where(hit, NEG_INF, score)
    return sel


def _lane_column(mat, j):
    lane = lax.broadcasted_iota(I32, mat.shape, 1)
    return jnp.sum(jnp.where(lane == j, mat, 0.0), axis=-1, keepdims=True)


def _online_update(s, v, m_i, l_i, acc):
    m_new = jnp.maximum(m_i, jnp.max(s, axis=-1, keepdims=True))
    alpha = jnp.exp(m_i - m_new)
    p = jnp.exp(s - m_new)
    l_new = alpha * l_i + jnp.sum(p, axis=-1, keepdims=True)
    acc_new = alpha * acc + jnp.dot(p.astype(BF16), v, preferred_element_type=F32)
    return m_new, l_new, acc_new


def _moba_prompt_kernel(slopes_ref, q_ref, k_ref, v_ref, o_ref, kb_s, vb_s, km_s, *, seq, scale):
    h = pl.program_id(1)
    qi = pl.program_id(2)
    nb = seq // MOBA_BLOCK
    tq = MOBA_BLOCK

    @pl.when(qi == 0)
    def _():
        km_s[...] = jnp.zeros(km_s.shape, F32)
        for b in range(nb):
            sl = slice(b * MOBA_BLOCK, (b + 1) * MOBA_BLOCK)
            kf = k_ref[sl, :]
            kb_s[sl, :] = kf.astype(BF16)
            vb_s[sl, :] = v_ref[sl, :].astype(BF16)
            km_s[b:b + 1, :] = jnp.mean(kf, axis=0, keepdims=True)

    q = q_ref[...]
    score = lax.dot_general(q.astype(F32), km_s[...], _NT, precision=lax.Precision.HIGHEST,
                            preferred_element_type=F32)
    lane = lax.broadcasted_iota(I32, score.shape, 1)
    score = jnp.where(lane < qi, score, NEG_INF)
    sel = _top_lanes(score, MOBA_TOPK)

    slope = slopes_ref[h]
    rr = lax.broadcasted_iota(I32, (tq, tq), 0)
    cc = lax.broadcasted_iota(I32, (tq, tq), 1)
    d0 = (rr - cc).astype(F32)

    def block_logits(j):
        off = pl.multiple_of(j * MOBA_BLOCK, MOBA_BLOCK)
        kj = kb_s[pl.ds(off, MOBA_BLOCK), :]
        vj = vb_s[pl.ds(off, MOBA_BLOCK), :]
        s = lax.dot_general(q, kj, _NT, preferred_element_type=F32) * scale
        s = s - slope * (d0 + ((qi - j) * MOBA_BLOCK).astype(F32))
        return s, vj

    def past_block(j, carry):
        s, vj = block_logits(j)
        s = jnp.where(_lane_column(sel, j) > 0.5, s, NEG_INF)
        return _online_update(s, vj, *carry)

    init = (jnp.full((tq, 1), NEG_INF, F32), jnp.zeros((tq, 1), F32), jnp.zeros((tq, LANES), F32))
    carry = lax.fori_loop(0, qi, past_block, init)
    s, vj = block_logits(qi)
    s = jnp.where(d0 >= 0.0, s, NEG_INF)
    _, l_i, acc = _online_update(s, vj, *carry)
    o_ref[...] = (acc / l_i).astype(o_ref.dtype)


def _moba_prompt(q, k, v, n_seq, n_heads):
    t, d = q.shape
    seq = t // n_seq
    hd = d // n_heads
    nq = seq // MOBA_BLOCK
    qspec = pl.BlockSpec((MOBA_BLOCK, hd), lambda n, h, i: (n * nq + i, h))
    kspec = pl.BlockSpec((seq, hd), lambda n, h, i: (n, h))
    return pl.pallas_call(
        functools.partial(_moba_prompt_kernel, seq=seq, scale=hd ** -0.5),
        out_shape=jax.ShapeDtypeStruct((t, d), BF16),
        grid=(n_seq, n_heads, nq),
        in_specs=[pl.BlockSpec(memory_space=pltpu.SMEM), qspec, kspec, kspec],
        out_specs=qspec,
        scratch_shapes=[pltpu.VMEM((seq, hd), BF16), pltpu.VMEM((seq, hd), BF16), pltpu.VMEM((LANES, hd), F32)],
        compiler_params=_params("arbitrary", "arbitrary", "arbitrary"),
        name="moba_prompt",
    )(_alibi_slopes(n_heads), q, k, v)


def _lambda_value(lam_ref, lam_init):
    l = lam_ref[...]
    a = jnp.exp(jnp.sum(l[0:1] * l[1:2], axis=-1, keepdims=True))
    b = jnp.exp(jnp.sum(l[2:3] * l[3:4], axis=-1, keepdims=True))
    return a - b + lam_init


def _sub_norm(o, gsub_ref, lam_init):
    y = o * lax.rsqrt(jnp.mean(o * o, axis=-1, keepdims=True) + RMS_EPS)
    return (y * gsub_ref[...]) * (1.0 - lam_init)


def _diff_prompt_kernel(slopes_ref, lam_ref, gsub_ref, q_ref, k_ref, v_ref, o_ref, kb_s, vb_s, *,
                        seq, tq, scale, lam_init):
    h = pl.program_id(1)
    qi = pl.program_id(2)
    hd = q_ref.shape[1] // 2

    @pl.when(qi == 0)
    def _():
        for b in range(seq // tq):
            sl = slice(b * tq, (b + 1) * tq)
            kb_s[sl, :] = k_ref[sl, :].astype(BF16)
            vb_s[sl, :] = v_ref[sl, :].astype(BF16)

    q = q_ref[...]
    slope = slopes_ref[h]
    rr = lax.broadcasted_iota(I32, (tq, tq), 0)
    cc = lax.broadcasted_iota(I32, (tq, tq), 1)
    d0 = (rr - cc).astype(F32)

    def block_update(j, carry, diagonal):
        off = pl.multiple_of(j * tq, tq)
        kj = kb_s[pl.ds(off, tq), :]
        vj = vb_s[pl.ds(off, tq), :]
        bias = slope * (d0 + ((qi - j) * tq).astype(F32))
        out = []
        for c in range(2):
            s = lax.dot_general(q[:, c * hd:(c + 1) * hd], kj[:, c * hd:(c + 1) * hd], _NT,
                                preferred_element_type=F32) * scale - bias
            if diagonal:
                s = jnp.where(d0 >= 0.0, s, NEG_INF)
            out.extend(_online_update(s, vj, *carry[3 * c:3 * c + 3]))
        return tuple(out)

    one = (jnp.full((tq, 1), NEG_INF, F32), jnp.zeros((tq, 1), F32), jnp.zeros((tq, 2 * hd), F32))
    carry = lax.fori_loop(0, qi, lambda j, c: block_update(j, c, False), one + one)
    _, l0, a0, _, l1, a1 = block_update(qi, carry, True)
    o = a0 / l0 - _lambda_value(lam_ref, lam_init) * (a1 / l1)
    o_ref[...] = _sub_norm(o, gsub_ref, lam_init).astype(o_ref.dtype)


def _diff_prompt(q, k, v, lam_vecs, g_sub, n_seq, n_heads, lam_init):
    t, d = q.shape
    seq = t // n_seq
    hw = d // n_heads
    tq = 256
    nq = seq // tq
    qspec = pl.BlockSpec((tq, hw), lambda n, h, i: (n * nq + i, h))
    kspec = pl.BlockSpec((seq, hw), lambda n, h, i: (n, h))
    return pl.pallas_call(
        functools.partial(_diff_prompt_kernel, seq=seq, tq=tq, scale=(hw // 2) ** -0.5, lam_init=lam_init),
        out_shape=jax.ShapeDtypeStruct((t, d), BF16),
        grid=(n_seq, n_heads, nq),
        in_specs=[pl.BlockSpec(memory_space=pltpu.SMEM),
                  pl.BlockSpec((4, hw // 2), lambda n, h, i: (0, 0)),
                  pl.BlockSpec((1, hw), lambda n, h, i: (0, 0)),
                  qspec, kspec, kspec],
        out_specs=qspec,
        scratch_shapes=[pltpu.VMEM((seq, hw), BF16), pltpu.VMEM((seq, hw), BF16)],
        compiler_params=_params("arbitrary", "arbitrary", "arbitrary"),
        name="diff_prompt",
    )(_alibi_slopes(n_heads), lam_vecs, g_sub.reshape(1, hw), q, k, v)


def _expand_queries(q, row_group, n_rep):
    d = q.shape[1]
    col_group = lax.broadcasted_iota(I32, (row_group.shape[0], d), 1) // LANES
    return jnp.where(col_group == row_group, n_rep, 0.0)


def _diag_select(o_full, row_head, width):
    acc = jnp.zeros((o_full.shape[0], width), F32)
    for hh in range(o_full.shape[1] // width):
        acc = acc + jnp.where(row_head == hh, o_full[:, hh * width:(hh + 1) * width], 0.0)
    return acc


def _moba_sample_kernel(pt_ref, q_ref, kn_ref, vn_ref, kc_ref, vc_ref, slope_ref, o_ref,
                        qx_s, m_s, l_s, acc_s, sc_s, *, n_heads, n_new, past, page, scale):
    p = pl.program_id(1)
    n_pages = pl.num_programs(1)
    rows = n_new * n_heads
    hd = o_ref.shape[1]
    row = lax.broadcasted_iota(I32, (rows, 1), 0)
    row_t = row // n_heads
    row_h = row % n_heads

    @pl.when(p == 0)
    def _():
        q = q_ref[...]
        rep = jnp.concatenate([jnp.broadcast_to(q[t:t + 1, :], (n_heads, q.shape[1])) for t in range(n_new)], axis=0)
        qx_s[...] = _expand_queries(q, row_h, rep)

    qx = qx_s[...]
    qxb = qx.astype(BF16)
    slope = slope_ref[...]
    lane = lax.broadcasted_iota(I32, (rows, page), 1)

    kp = kc_ref[...]
    s = lax.dot_general(qxb, kp.astype(BF16), _NT, preferred_element_type=F32) * scale
    dist = (past + row_t - p * page - lane).astype(F32)
    s = s - slope * dist
    m = jnp.max(s, axis=-1, keepdims=True)
    e = jnp.exp(s - m)
    o_full = jnp.dot(e.astype(BF16), vc_ref[...].astype(BF16), preferred_element_type=F32)
    m_s[p] = jnp.broadcast_to(m, (rows, LANES))
    l_s[p] = jnp.broadcast_to(jnp.sum(e, axis=-1, keepdims=True), (rows, LANES))
    acc_s[p] = _diag_select(o_full, row_h, hd)
    ksum = jnp.sum(kp, axis=0, keepdims=True)
    sc_s[p] = jnp.broadcast_to(jnp.sum(qx * ksum, axis=-1, keepdims=True), (rows, LANES))

    @pl.when(p == n_pages - 1)
    def _():
        ppb = MOBA_BLOCK // page
        lane_b = lax.broadcasted_iota(I32, (rows, LANES), 1)

        def add_score(pp, sc):
            return jnp.where(lane_b == pp // ppb, sc + sc_s[pp], sc)

        score = lax.fori_loop(0, n_pages, add_score, jnp.zeros((rows, LANES), F32)) * (1.0 / MOBA_BLOCK)
        score = jnp.where(lane_b < past // MOBA_BLOCK, score, NEG_INF)
        sel = _top_lanes(score, MOBA_TOPK)

        kn = kn_ref[...]
        lane_n = lax.broadcasted_iota(I32, (rows, kn.shape[0]), 1)
        s_own = lax.dot_general(qxb, kn.astype(BF16), _NT, preferred_element_type=F32) * scale
        s_own = s_own - slope * (row_t - lane_n).astype(F32)
        s_own = jnp.where(lane_n <= row_t, s_own, NEG_INF)
        m_own = jnp.max(s_own, axis=-1, keepdims=True)

        def page_sel(pp):
            return _lane_column(sel, pp // ppb) > 0.5

        def max_body(pp, mx):
            return jnp.maximum(mx, jnp.where(page_sel(pp), m_s[pp], NEG_INF))

        m_fin = lax.fori_loop(0, n_pages, max_body, jnp.broadcast_to(m_own, (rows, LANES)))

        def acc_body(pp, carry):
            l_i, acc = carry
            w = jnp.where(page_sel(pp), jnp.exp(m_s[pp] - m_fin), 0.0)
            return l_i + w * l_s[pp], acc + w * acc_s[pp]

        e_own = jnp.exp(s_own - m_fin[:, 0:1])
        o_own = jnp.dot(e_own.astype(BF16), vn_ref[...].astype(BF16), preferred_element_type=F32)
        l0 = jnp.broadcast_to(jnp.sum(e_own, axis=-1, keepdims=True), (rows, LANES))
        l_fin, acc = lax.fori_loop(0, n_pages, acc_body, (l0, _diag_select(o_own, row_h, hd)))
        o_ref[...] = (acc / l_fin).astype(o_ref.dtype)


def _row_slopes(n_heads, row_heads):
    return _alibi_slopes(n_heads)[row_heads].reshape(-1, 1)


def _pad_new(x, n_seq, n_new):
    d = x.shape[1]
    return jnp.pad(x.reshape(n_seq, n_new, d), ((0, 0), (0, NEW_PAD - n_new), (0, 0)))


def _moba_sample(q, k_new, v_new, cache_k, cache_v, layer, page_table, n_heads):
    n_seq, n_pages = page_table.shape
    t, d = q.shape
    n_new = t // n_seq
    page = cache_k.shape[2]
    hd = d // n_heads
    rows = n_new * n_heads
    row_heads = jnp.arange(rows) % n_heads
    seq_spec = lambda r: pl.BlockSpec((None, r, d), lambda b, p, pt: (b, 0, 0))
    page_spec = pl.BlockSpec((None, None, page, d), lambda b, p, pt: (layer, pt[b, p], 0, 0))
    kernel = functools.partial(_moba_sample_kernel, n_heads=n_heads, n_new=n_new, past=n_pages * page,
                               page=page, scale=hd ** -0.5)
    out = pl.pallas_call(
        kernel,
        out_shape=jax.ShapeDtypeStruct((n_seq, rows, hd), BF16),
        grid_spec=pltpu.PrefetchScalarGridSpec(
            num_scalar_prefetch=1,
            grid=(n_seq, n_pages),
            in_specs=[seq_spec(n_new), seq_spec(NEW_PAD), seq_spec(NEW_PAD), page_spec, page_spec,
                      pl.BlockSpec((rows, 1), lambda b, p, pt: (0, 0))],
            out_specs=pl.BlockSpec((None, rows, hd), lambda b, p, pt: (b, 0, 0)),
            scratch_shapes=[pltpu.VMEM((rows, d), F32)] + [pltpu.VMEM((n_pages, rows, LANES), F32)] * 4,
        ),
        compiler_params=_params("arbitrary", "arbitrary"),
        name="moba_sample",
    )(page_table, q.astype(F32).reshape(n_seq, n_new, d), _pad_new(k_new, n_seq, n_new),
      _pad_new(v_new, n_seq, n_new), cache_k, cache_v, _row_slopes(n_heads, row_heads))
    return out.reshape(t, d)


def _diff_sample_kernel(pt_ref, q_ref, kn_ref, vn_ref, kc_ref, vc_ref, slope_ref, lam_ref, gsub_ref, o_ref,
                        qx_s, m_s, l_s, acc_s, *, n_heads, n_new, past, page, scale, lam_init):
    p = pl.program_id(1)
    n_pages = pl.num_programs(1)
    half = n_new * n_heads
    rows = 2 * half
    hw = o_ref.shape[1]
    row = lax.broadcasted_iota(I32, (rows, 1), 0)
    row_c = row // half
    row_t = (row % half) // n_heads
    row_h = row % n_heads

    @pl.when(p == 0)
    def _():
        q = q_ref[...]
        rep = jnp.concatenate([jnp.broadcast_to(q[t:t + 1, :], (n_heads, q.shape[1]))
                               for _ in range(2) for t in range(n_new)], axis=0)
        qx_s[...] = _expand_queries(q, row_h * 2 + row_c, rep).astype(BF16)
        m_s[...] = jnp.full(m_s.shape, NEG_INF, F32)
        l_s[...] = jnp.zeros(l_s.shape, F32)
        acc_s[...] = jnp.zeros(acc_s.shape, F32)

    qxb = qx_s[...]
    slope = slope_ref[...]

    def update(keys, values, dist, valid):
        s = lax.dot_general(qxb, keys.astype(BF16), _NT, preferred_element_type=F32) * scale
        s = s - slope * dist.astype(F32)
        if valid is not None:
            s = jnp.where(valid, s, NEG_INF)
        m_prev = m_s[...]
        m_new = jnp.maximum(m_prev, jnp.max(s, axis=-1, keepdims=True))
        alpha = jnp.exp(m_prev - m_new)
        e = jnp.exp(s - m_new[:, 0:1])
        o_full = jnp.dot(e.astype(BF16), values.astype(BF16), preferred_element_type=F32)
        m_s[...] = m_new
        l_s[...] = alpha * l_s[...] + jnp.sum(e, axis=-1, keepdims=True)
        acc_s[...] = alpha[:, 0:1] * acc_s[...] + _diag_select(o_full, row_h, hw)

    lane = lax.broadcasted_iota(I32, (rows, page), 1)
    update(kc_ref[...], vc_ref[...], past + row_t - p * page - lane, None)

    @pl.when(p == n_pages - 1)
    def _():
        kn = kn_ref[...]
        lane_n = lax.broadcasted_iota(I32, (rows, kn.shape[0]), 1)
        update(kn, vn_ref[...], row_t - lane_n, lane_n <= row_t)
        o_c = acc_s[...] / l_s[:, 0:1]
        o = o_c[:half] - _lambda_value(lam_ref, lam_init) * o_c[half:]
        o_ref[...] = _sub_norm(o, gsub_ref, lam_init).astype(o_ref.dtype)


def _diff_sample(q, k_new, v_new, cache_k, cache_v, layer, page_table, n_heads, lam_vecs, g_sub, lam_init):
    n_seq, n_pages = page_table.shape
    t, d = q.shape
    n_new = t // n_seq
    page = cache_k.shape[2]
    hw = d // n_heads
    half = n_new * n_heads
    rows = 2 * half
    row_heads = jnp.arange(rows) % n_heads
    seq_spec = lambda r: pl.BlockSpec((None, r, d), lambda b, p, pt: (b, 0, 0))
    page_spec = pl.BlockSpec((None, None, page, d), lambda b, p, pt: (layer, pt[b, p], 0, 0))
    kernel = functools.partial(_diff_sample_kernel, n_heads=n_heads, n_new=n_new, past=n_pages * page,
                               page=page, scale=(hw // 2) ** -0.5, lam_init=lam_init)
    out = pl.pallas_call(
        kernel,
        out_shape=jax.ShapeDtypeStruct((n_seq, half, hw), BF16),
        grid_spec=pltpu.PrefetchScalarGridSpec(
            num_scalar_prefetch=1,
            grid=(n_seq, n_pages),
            in_specs=[seq_spec(n_new), seq_spec(NEW_PAD), seq_spec(NEW_PAD), page_spec, page_spec,
                      pl.BlockSpec((rows, 1), lambda b, p, pt: (0, 0)),
                      pl.BlockSpec((4, hw // 2), lambda b, p, pt: (0, 0)),
                      pl.BlockSpec((1, hw), lambda b, p, pt: (0, 0))],
            out_specs=pl.BlockSpec((None, half, hw), lambda b, p, pt: (b, 0, 0)),
            scratch_shapes=[pltpu.VMEM((rows, d), BF16), pltpu.VMEM((rows, LANES), F32),
                            pltpu.VMEM((rows, LANES), F32), pltpu.VMEM((rows, hw), F32)],
        ),
        compiler_params=_params("arbitrary", "arbitrary"),
        name="diff_sample",
    )(page_table, q.astype(F32).reshape(n_seq, n_new, d), _pad_new(k_new, n_seq, n_new),
      _pad_new(v_new, n_seq, n_new), cache_k, cache_v, _row_slopes(n_heads, row_heads), lam_vecs,
      g_sub.reshape(1, hw))
    return out.reshape(t, d)


def _first_index_of_max(vals, idx, n):
    m = jnp.max(vals, axis=0, keepdims=True)
    first = jnp.min(jnp.where(vals == m, idx, float(n)), axis=0, keepdims=True)
    return m, first


def _ffn_prologue_kernel(x_ref, g_ref, sh_ref, sc_ref, wr_ref, br_ref, h_ref, idx_ref, gw_ref):
    h = _modnorm_value(x_ref[...], g_ref[...], sh_ref[...], sc_ref[...])
    h_ref[...] = h.astype(h_ref.dtype)
    logits = lax.dot_general(wr_ref[...], h, _NT, precision=lax.Precision.HIGHEST, preferred_element_type=F32)
    s = jax.nn.sigmoid(logits)
    choice = s + br_ref[...]
    n_exp, tm = s.shape
    per = n_exp // N_GROUPS
    sub = lax.broadcasted_iota(I32, (per, tm), 0).astype(F32)
    grp_rows = []
    for g in range(N_GROUPS):
        cg = choice[g * per:(g + 1) * per, :]
        m1, first = _first_index_of_max(cg, sub, per)
        m2 = jnp.max(jnp.where(sub == first, NEG_INF, cg), axis=0, keepdims=True)
        grp_rows.append(m1 + m2)
    grp = jnp.concatenate(grp_rows, axis=0)
    gi = lax.broadcasted_iota(I32, (N_GROUPS, tm), 0).astype(F32)
    keep = jnp.zeros((N_GROUPS, tm), F32)
    for _ in range(TOPK_GROUPS):
        _, first = _first_index_of_max(grp, gi, N_GROUPS)
        hit = gi == first
        keep = jnp.where(hit, 1.0, keep)
        grp = jnp.where(hit, NEG_INF, grp)
    masked = jnp.concatenate(
        [jnp.where(keep[g:g + 1, :] > 0.5, choice[g * per:(g + 1) * per, :], NEG_INF) for g in range(N_GROUPS)],
        axis=0)
    ei = lax.broadcasted_iota(I32, (n_exp, tm), 0).astype(F32)
    idx_rows, w_rows = [], []
    for _ in range(TOP_K):
        _, first = _first_index_of_max(masked, ei, n_exp)
        hit = ei == first
        idx_rows.append(first)
        w_rows.append(jnp.sum(jnp.where(hit, s, 0.0), axis=0, keepdims=True))
        masked = jnp.where(hit, NEG_INF, masked)
    w = jnp.concatenate(w_rows, axis=0)
    idx_ref[...] = jnp.concatenate(idx_rows, axis=0).astype(I32)
    gw_ref[...] = w / jnp.sum(w, axis=0, keepdims=True) * ROUTED_SCALE


def _ffn_prologue(x, g, mod, w_router_t, b_router, tm):
    t, d = x.shape
    n_exp = w_router_t.shape[0]
    return pl.pallas_call(
        _ffn_prologue_kernel,
        out_shape=(jax.ShapeDtypeStruct((t, d), BF16), jax.ShapeDtypeStruct((TOP_K, t), I32),
                   jax.ShapeDtypeStruct((TOP_K, t), F32)),
        grid=(t // tm,),
        in_specs=[pl.BlockSpec((tm, d), lambda i: (i, 0)),
                  pl.BlockSpec((1, d), lambda i: (0, 0)),
                  mod.spec(0, tm, d, d, 0),
                  mod.spec(1, tm, d, d, 0),
                  pl.BlockSpec((n_exp, d), lambda i: (0, 0)),
                  pl.BlockSpec((n_exp, 1), lambda i: (0, 0))],
        out_specs=(pl.BlockSpec((tm, d), lambda i: (i, 0)),
                   pl.BlockSpec((TOP_K, tm), lambda i: (0, i)),
                   pl.BlockSpec((TOP_K, tm), lambda i: (0, i))),
        compiler_params=_params("arbitrary"),
        name="ffn_prologue",
    )(x, g.reshape(1, d), mod.arr, mod.arr, w_router_t, b_router.reshape(n_exp, 1))


def _expert_kernel(be_ref, nv_ref, x_ref, wg_ref, wu_ref, wd_ref, y_ref, wg_s, wu_s, wd_s):
    i = pl.program_id(0)
    changed = jnp.logical_or(i == 0, be_ref[i] != be_ref[jnp.maximum(i - 1, 0)])
    live = i < nv_ref[0]

    @pl.when(jnp.logical_and(live, changed))
    def _():
        wg_s[...] = wg_ref[...].astype(BF16)
        wu_s[...] = wu_ref[...].astype(BF16)
        wd_s[...] = wd_ref[...].astype(BF16)

    @pl.when(live)
    def _():
        x = x_ref[...]
        gate = jnp.dot(x, wg_s[...], preferred_element_type=F32)
        up = jnp.dot(x, wu_s[...], preferred_element_type=F32)
        a = (gate * jax.nn.sigmoid(gate) * up).astype(BF16)
        y_ref[...] = jnp.dot(a, wd_s[...], preferred_element_type=F32)

    @pl.when(jnp.logical_not(live))
    def _():
        y_ref[...] = jnp.zeros(y_ref.shape, y_ref.dtype)


def _experts(x, block_e, n_valid, wg, wu, wd, layer, bm):
    n, d = x.shape
    f = wg.shape[-1]
    n_blocks = n // bm
    up_spec = pl.BlockSpec((None, None, d, f), lambda i, be, nv: (layer, be[i], 0, 0))
    down_spec = pl.BlockSpec((None, None, f, d), lambda i, be, nv: (layer, be[i], 0, 0))
    return pl.pallas_call(
        _expert_kernel,
        out_shape=jax.ShapeDtypeStruct((n, d), F32),
        grid_spec=pltpu.PrefetchScalarGridSpec(
            num_scalar_prefetch=2,
            grid=(n_blocks,),
            in_specs=[pl.BlockSpec((bm, d), lambda i, be, nv: (i, 0)), up_spec, up_spec, down_spec],
            out_specs=pl.BlockSpec((bm, d), lambda i, be, nv: (i, 0)),
            scratch_shapes=[pltpu.VMEM((d, f), BF16), pltpu.VMEM((d, f), BF16), pltpu.VMEM((f, d), BF16)],
        ),
        compiler_params=_params("arbitrary"),
        name="experts",
    )(block_e, n_valid, x, wg, wu, wd)


def _dispatch_plan(idx_t, n_exp, bm):
    k, t = idx_t.shape
    a = t * k
    flat_e = idx_t.T.reshape(a)
    order = jnp.argsort(flat_e)
    e_sorted = flat_e[order]
    counts = jnp.bincount(flat_e, length=n_exp)
    padded = (counts + bm - 1) // bm * bm
    pad_end = jnp.cumsum(padded)
    pad_start = pad_end - padded
    start = jnp.cumsum(counts) - counts
    dest = (pad_start[e_sorted] + jnp.arange(a) - start[e_sorted]).astype(I32)
    n_blocks = -(-a // bm) + n_exp
    slot_tok = jnp.zeros((n_blocks * bm,), I32).at[dest].set((order // k).astype(I32))
    block_e = jnp.minimum(jnp.searchsorted(pad_end, jnp.arange(n_blocks) * bm, side='right'), n_exp - 1).astype(I32)
    n_valid = (pad_end[-1:] // bm).astype(I32)
    pos = jnp.zeros((a,), I32).at[order].set(dest).reshape(t, k)
    return slot_tok, block_e, n_valid, pos


def _ffn_out_kernel(x_ref, gate_ref, routed_ref, shared_ref, o_ref):
    o_ref[...] = x_ref[...] + gate_ref[...] * (routed_ref[...] + shared_ref[...])


def _ffn_out(x, mod, routed, shared, row_off, tm):
    t, d = x.shape
    return pl.pallas_call(
        _ffn_out_kernel,
        out_shape=jax.ShapeDtypeStruct((t, d), F32),
        grid=(t // tm,),
        in_specs=[pl.BlockSpec((tm, d), lambda i: (i, 0)),
                  mod.spec(2, tm, d, d, 0),
                  pl.BlockSpec((tm, d), lambda i: (row_off + i, 0)),
                  pl.BlockSpec((tm, d), lambda i: (row_off + i, 0))],
        out_specs=pl.BlockSpec((tm, d), lambda i: (i, 0)),
        compiler_params=_params("arbitrary"),
        name="ffn_out",
    )(x, mod.arr, routed, shared)


def kernel(x_prompt, x_sample, cache_k_moba, cache_v_moba, cache_k_diff, cache_v_diff, page_table, c_prompt, c_sample, g_mix, w_ada_mix, b_ada_mix, w_qkv_moba, g_q_moba, g_k_moba, w_o_moba, w_qkv_diff, g_q_diff, g_k_diff, lambda_q1, lambda_k1, lambda_q2, lambda_k2, g_sub_diff, w_o_diff, g_ffn, w_ada_ffn, b_ada_ffn, w_router, b_router, w_gate_exp, w_up_exp, w_down_exp, w_gate_sh, w_up_sh, w_down_sh):
    n_p, seq, d = x_prompt.shape
    n_s, n_new, _ = x_sample.shape
    depth = g_mix.shape[0]
    t_p, t_s = n_p * seq, n_s * n_new
    moba_heads = cache_k_moba.shape[3]
    diff_heads = cache_k_diff.shape[3]
    page = cache_k_moba.shape[2]
    n_exp = w_router.shape[2]
    tm_p = 512
    bm = 256

    xp = x_prompt.reshape(t_p, d)
    xs = x_sample.reshape(t_s, d)

    c_all = jnp.concatenate([c_prompt, c_sample], axis=0)
    n_c = c_all.shape[0]
    c_all = jnp.pad(c_all, ((0, -n_c % 8), (0, 0)))
    ada_mix = _ada(c_all, w_ada_mix, b_ada_mix)
    ada_ffn = _ada(c_all, w_ada_ffn, b_ada_ffn)

    def mods(ada, i):
        mp = _Mod(ada[i, :n_p].reshape(n_p, 1, 3 * d), seq)
        ms = _Mod(jnp.repeat(ada[i, n_p:n_p + n_s], n_new, axis=0), n_new)
        return mp, ms

    ck_m = cache_k_moba.reshape(cache_k_moba.shape[:3] + (d,))
    cv_m = cache_v_moba.reshape(cache_v_moba.shape[:3] + (d,))
    ck_d = cache_k_diff.reshape(cache_k_diff.shape[:3] + (d,))
    cv_d = cache_v_diff.reshape(cache_v_diff.shape[:3] + (d,))

    kv = {name: [] for name in ("k_mp", "v_mp", "k_dp", "v_dp", "k_ms", "v_ms", "k_ds", "v_ds")}
    for i in range(depth):
        j = i // 2
        mod_p, mod_s = mods(ada_mix, i)
        hp = _modnorm(xp, g_mix[i], mod_p, tm_p)
        hs = _modnorm(xs, g_mix[i], mod_s, t_s)
        if i % 2 == 0:
            gq = jnp.tile(g_q_moba[j], moba_heads)
            gk = jnp.tile(g_k_moba[j], moba_heads)
            qp, kp, vp = _qkv(hp, w_qkv_moba, j, gq, gk, tm_p)
            qs, ks, vs = _qkv(hs, w_qkv_moba, j, gq, gk, t_s)
            op = _moba_prompt(qp, kp, vp, n_p, moba_heads)
            os_ = _moba_sample(qs, ks, vs, ck_m, cv_m, j, page_table, moba_heads)
            w_o = w_o_moba
            kv["k_mp"].append(kp.reshape(n_p, seq, moba_heads, d // moba_heads))
            kv["v_mp"].append(vp.reshape(n_p, seq, moba_heads, d // moba_heads))
            kv["k_ms"].append(ks.reshape(n_s, n_new, moba_heads, d // moba_heads))
            kv["v_ms"].append(vs.reshape(n_s, n_new, moba_heads, d // moba_heads))
        else:
            lam_init = 0.8 - 0.6 * math.exp(-0.3 * i)
            lam_vecs = jnp.stack([lambda_q1[j], lambda_k1[j], lambda_q2[j], lambda_k2[j]]).astype(F32)
            gq = jnp.tile(g_q_diff[j].reshape(-1), diff_heads)
            gk = jnp.tile(g_k_diff[j].reshape(-1), diff_heads)
            qp, kp, vp = _qkv(hp, w_qkv_diff, j, gq, gk, tm_p)
            qs, ks, vs = _qkv(hs, w_qkv_diff, j, gq, gk, t_s)
            op = _diff_prompt(qp, kp, vp, lam_vecs, g_sub_diff[j], n_p, diff_heads, lam_init)
            os_ = _diff_sample(qs, ks, vs, ck_d, cv_d, j, page_table, diff_heads, lam_vecs, g_sub_diff[j], lam_init)
            w_o = w_o_diff
            hw = d // diff_heads
            kv["k_dp"].append(kp.reshape(n_p, seq, diff_heads, 2, hw // 2))
            kv["v_dp"].append(vp.reshape(n_p, seq, diff_heads, hw))
            kv["k_ds"].append(ks.reshape(n_s, n_new, diff_heads, 2, hw // 2))
            kv["v_ds"].append(vs.reshape(n_s, n_new, diff_heads, hw))
        xp = _oproj(op, w_o, j, xp, mod_p, tm_p)
        xs = _oproj(os_, w_o, j, xs, mod_s, t_s)

        mod_p, mod_s = mods(ada_ffn, i)
        w_rt = w_router[i].T
        hp, idx_p, gw_p = _ffn_prologue(xp, g_ffn[i], mod_p, w_rt, b_router[i], 256)
        hs, idx_s, gw_s = _ffn_prologue(xs, g_ffn[i], mod_s, w_rt, b_router[i], t_s)
        h_all = jnp.concatenate([hp, hs], axis=0)
        idx_t = jnp.concatenate([idx_p, idx_s], axis=1)
        gw = jnp.concatenate([gw_p, gw_s], axis=1).T
        slot_tok, block_e, n_valid, pos = _dispatch_plan(idx_t, n_exp, bm)
        yb = _experts(h_all[slot_tok], block_e, n_valid, w_gate_exp, w_up_exp, w_down_exp, i, bm)
        routed = jnp.einsum('tkd,tk->td', yb[pos], gw)
        t_all = t_p + t_s
        shared = _experts(h_all, jnp.zeros((t_all // t_s,), I32), jnp.full((1,), t_all // t_s, I32),
                          w_gate_sh[:, None], w_up_sh[:, None], w_down_sh[:, None], i, t_s)
        xp = _ffn_out(xp, mod_p, routed, shared, 0, tm_p)
        xs = _ffn_out(xs, mod_s, routed, shared, t_p // t_s, t_s)

    return (xp.reshape(n_p, seq, d), xs.reshape(n_s, n_new, d),
            jnp.stack(kv["k_mp"]), jnp.stack(kv["v_mp"]), jnp.stack(kv["k_dp"]), jnp.stack(kv["v_dp"]),
            jnp.stack(kv["k_ms"]), jnp.stack(kv["v_ms"]), jnp.stack(kv["k_ds"]), jnp.stack(kv["v_ds"]))
```

```python
import functools
import math

import jax
import jax.numpy as jnp
from jax import lax
from jax.experimental import pallas as pl
from jax.experimental.pallas import tpu as pltpu

F32 = jnp.float32
BF16 = jnp.bfloat16
I32 = jnp.int32

MOBA_BLOCK = 256
MOBA_TOPK = 3
N_GROUPS = 8
TOPK_GROUPS = 4
TOP_K = 8
ROUTED_SCALE = 2.5
RMS_EPS = 1e-6
NEG_INF = -1e30

LANES = 128
VMEM_LIMIT = 48 * 1024 * 1024

_NT = (((1,), (1,)), ((), ()))


def _params(*sem):
    return pltpu.CompilerParams(dimension_semantics=sem, vmem_limit_bytes=VMEM_LIMIT)


def _alibi_slopes(n_heads):
    return jnp.asarray([2.0 ** (-8.0 * (h + 1) / n_heads) for h in range(n_heads)], dtype=F32)


class _Mod:
    def __init__(self, arr, rows_per_seq):
        self.arr = arr
        self.rows_per_seq = rows_per_seq
        self.per_token = arr.ndim == 2

    def spec(self, part, tm, d, tn, row_axis, col_axis=None):
        ncol = d // tn

        def col(g):
            return part * ncol + (g[col_axis] if col_axis is not None else 0)

        if self.per_token:
            return pl.BlockSpec((tm, tn), lambda *g: (g[row_axis], col(g)))
        tps = self.rows_per_seq // tm
        return pl.BlockSpec((None, 1, tn), lambda *g: (g[row_axis] // tps, 0, col(g)))


def _ada_kernel(c_ref, w_ref, b_ref, o_ref):
    c = c_ref[...]
    a = (c * jax.nn.sigmoid(c)).astype(BF16)
    o_ref[...] = jnp.dot(a, w_ref[...].astype(BF16), preferred_element_type=F32) + b_ref[...]


def _ada(c_all, w, b):
    depth, d, n3 = w.shape
    mp = c_all.shape[0]
    tn = 512
    return pl.pallas_call(
        _ada_kernel,
        out_shape=jax.ShapeDtypeStruct((depth, mp, n3), F32),
        grid=(depth, n3 // tn),
        in_specs=[pl.BlockSpec((mp, d), lambda l, j: (0, 0)),
                  pl.BlockSpec((None, d, tn), lambda l, j: (l, 0, j)),
                  pl.BlockSpec((None, 1, tn), lambda l, j: (l, 0, j))],
        out_specs=pl.BlockSpec((None, mp, tn), lambda l, j: (l, 0, j)),
        compiler_params=_params("arbitrary", "arbitrary"),
        name="ada",
    )(c_all, w, b.reshape(depth, 1, n3))


def _modnorm_value(x, g, shift, scale):
    y = x * lax.rsqrt(jnp.mean(x * x, axis=-1, keepdims=True) + RMS_EPS)
    return (y * g) * (1.0 + scale) + shift


def _modnorm_kernel(x_ref, g_ref, sh_ref, sc_ref, h_ref):
    h_ref[...] = _modnorm_value(x_ref[...], g_ref[...], sh_ref[...], sc_ref[...]).astype(h_ref.dtype)


def _modnorm(x, g, mod, tm):
    t, d = x.shape
    return pl.pallas_call(
        _modnorm_kernel,
        out_shape=jax.ShapeDtypeStruct((t, d), BF16),
        grid=(t // tm,),
        in_specs=[pl.BlockSpec((tm, d), lambda i: (i, 0)),
                  pl.BlockSpec((1, d), lambda i: (0, 0)),
                  mod.spec(0, tm, d, d, 0),
                  mod.spec(1, tm, d, d, 0)],
        out_specs=pl.BlockSpec((tm, d), lambda i: (i, 0)),
        compiler_params=_params("arbitrary"),
        name="modnorm",
    )(x, g.reshape(1, d), mod.arr, mod.arr)


def _group_norm_store(y, g_ref, o_ref):
    for c in range(y.shape[1] // LANES):
        sl = slice(c * LANES, (c + 1) * LANES)
        yc = y[:, sl]
        yc = yc * lax.rsqrt(jnp.mean(yc * yc, axis=-1, keepdims=True) + RMS_EPS) * g_ref[:, sl]
        o_ref[:, sl] = yc.astype(o_ref.dtype)


def _qkv_kernel(h_ref, wq_ref, wk_ref, wv_ref, gq_ref, gk_ref, q_ref, k_ref, v_ref, wq_s, wk_s, wv_s):
    @pl.when(pl.program_id(1) == 0)
    def _():
        wq_s[...] = wq_ref[...].astype(BF16)
        wk_s[...] = wk_ref[...].astype(BF16)
        wv_s[...] = wv_ref[...].astype(BF16)

    h = h_ref[...]
    _group_norm_store(jnp.dot(h, wq_s[...], preferred_element_type=F32), gq_ref, q_ref)
    _group_norm_store(jnp.dot(h, wk_s[...], preferred_element_type=F32), gk_ref, k_ref)
    v_ref[...] = jnp.dot(h, wv_s[...], preferred_element_type=F32)


def _qkv(h, w, layer, gq, gk, tm):
    t, d = h.shape
    tn = 256
    nj = d // tn
    wspec = lambda part: pl.BlockSpec((None, d, tn), lambda j, i: (layer, 0, part * nj + j))
    gspec = pl.BlockSpec((1, tn), lambda j, i: (0, j))
    ospec = pl.BlockSpec((tm, tn), lambda j, i: (i, j))
    return pl.pallas_call(
        _qkv_kernel,
        out_shape=(jax.ShapeDtypeStruct((t, d), BF16), jax.ShapeDtypeStruct((t, d), F32),
                   jax.ShapeDtypeStruct((t, d), F32)),
        grid=(nj, t // tm),
        in_specs=[pl.BlockSpec((tm, d), lambda j, i: (i, 0)), wspec(0), wspec(1), wspec(2), gspec, gspec],
        out_specs=(ospec, ospec, ospec),
        scratch_shapes=[pltpu.VMEM((d, tn), BF16)] * 3,
        compiler_params=_params("arbitrary", "arbitrary"),
        name="qkv",
    )(h, w, w, w, gq.reshape(1, d), gk.reshape(1, d))


def _oproj_kernel(o_ref, w_ref, x_ref, gate_ref, out_ref, w_s):
    @pl.when(pl.program_id(1) == 0)
    def _():
        w_s[...] = w_ref[...].astype(BF16)

    out_ref[...] = x_ref[...] + gate_ref[...] * jnp.dot(o_ref[...], w_s[...], preferred_element_type=F32)


def _oproj(o, w, layer, x, mod, tm):
    t, d = o.shape
    tn = 512
    return pl.pallas_call(
        _oproj_kernel,
        out_shape=jax.ShapeDtypeStruct((t, d), F32),
        grid=(d // tn, t // tm),
        in_specs=[pl.BlockSpec((tm, d), lambda j, i: (i, 0)),
                  pl.BlockSpec((None, d, tn), lambda j, i: (layer, 0, j)),
                  pl.BlockSpec((tm, tn), lambda j, i: (i, j)),
                  mod.spec(2, tm, d, tn, 1, 0)],
        out_specs=pl.BlockSpec((tm, tn), lambda j, i: (i, j)),
        scratch_shapes=[pltpu.VMEM((d, tn), BF16)],
        compiler_params=_params("arbitrary", "arbitrary"),
        name="oproj",
    )(o, w, x, mod.arr)


def _top_lanes(score, n_pick):
    lane = lax.broadcasted_iota(I32, score.shape, 1).astype(F32)
    sel = jnp.zeros(score.shape, F32)
    for _ in range(n_pick):
        m = jnp.max(score, axis=-1, keepdims=True)
        first = jnp.min(jnp.where(score == m, lane, float(LANES)), axis=-1, keepdims=True)
        hit = lane == first
        sel = jnp.where(hit & (m > 0.5 * NEG_INF), 1.0, sel)
        score = jnp.where(hit, NEG_INF, score)
    return sel


def _lane_column(mat, j):
    lane = lax.broadcasted_iota(I32, mat.shape, 1)
    return jnp.sum(jnp.where(lane == j, mat, 0.0), axis=-1, keepdims=True)


def _online_update(s, v, m_i, l_i, acc):
    m_new = jnp.maximum(m_i, jnp.max(s, axis=-1, keepdims=True))
    alpha = jnp.exp(m_i - m_new)
    p = jnp.exp(s - m_new)
    l_new = alpha * l_i + jnp.sum(p, axis=-1, keepdims=True)
    acc_new = alpha * acc + jnp.dot(p.astype(BF16), v, preferred_element_type=F32)
    return m_new, l_new, acc_new


def _moba_prompt_kernel(slopes_ref, q_ref, k_ref, v_ref, o_ref, kb_s, vb_s, km_s, *, seq, scale):
    h = pl.program_id(1)
    qi = pl.program_id(2)
    nb = seq // MOBA_BLOCK
    tq = MOBA_BLOCK

    @pl.when(qi == 0)
    def _():
        km_s[...] = jnp.zeros(km_s.shape, F32)
        for b in range(nb):
            sl = slice(b * MOBA_BLOCK, (b + 1) * MOBA_BLOCK)
            kf = k_ref[sl, :]
            kb_s[sl, :] = kf.astype(BF16)
            vb_s[sl, :] = v_ref[sl, :].astype(BF16)
            km_s[b:b + 1, :] = jnp.mean(kf, axis=0, keepdims=True)

    q = q_ref[...]
    score = lax.dot_general(q.astype(F32), km_s[...], _NT, precision=lax.Precision.HIGHEST,
                            preferred_element_type=F32)
    lane = lax.broadcasted_iota(I32, score.shape, 1)
    score = jnp.where(lane < qi, score, NEG_INF)
    sel = _top_lanes(score, MOBA_TOPK)

    slope = slopes_ref[h]
    n_chain = 2
    tr = tq // n_chain
    rr = lax.broadcasted_iota(I32, (tr, tq), 0)
    cc = lax.broadcasted_iota(I32, (tr, tq), 1)
    d0 = [(rr + c * tr - cc).astype(F32) for c in range(n_chain)]
    qc = [q[c * tr:(c + 1) * tr] for c in range(n_chain)]
    selc = [sel[c * tr:(c + 1) * tr] for c in range(n_chain)]

    def block_logits(j, c):
        off = pl.multiple_of(j * MOBA_BLOCK, MOBA_BLOCK)
        kj = kb_s[pl.ds(off, MOBA_BLOCK), :]
        vj = vb_s[pl.ds(off, MOBA_BLOCK), :]
        s = lax.dot_general(qc[c], kj, _NT, preferred_element_type=F32) * scale
        s = s - slope * (d0[c] + ((qi - j) * MOBA_BLOCK).astype(F32))
        return s, vj

    def past_block(j, carry):
        out = []
        for c in range(n_chain):
            s, vj = block_logits(j, c)
            s = jnp.where(_lane_column(selc[c], j) > 0.5, s, NEG_INF)
            out.extend(_online_update(s, vj, *carry[3 * c:3 * c + 3]))
        return tuple(out)

    one = (jnp.full((tr, 1), NEG_INF, F32), jnp.zeros((tr, 1), F32), jnp.zeros((tr, LANES), F32))
    carry = lax.fori_loop(0, qi, past_block, one * n_chain)
    for c in range(n_chain):
        s, vj = block_logits(qi, c)
        s = jnp.where(d0[c] >= 0.0, s, NEG_INF)
        _, l_i, acc = _online_update(s, vj, *carry[3 * c:3 * c + 3])
        o_ref[c * tr:(c + 1) * tr, :] = (acc / l_i).astype(o_ref.dtype)


def _moba_prompt(q, k, v, n_seq, n_heads):
    t, d = q.shape
    seq = t // n_seq
    hd = d // n_heads
    nq = seq // MOBA_BLOCK
    qspec = pl.BlockSpec((MOBA_BLOCK, hd), lambda n, h, i: (n * nq + i, h))
    kspec = pl.BlockSpec((seq, hd), lambda n, h, i: (n, h))
    return pl.pallas_call(
        functools.partial(_moba_prompt_kernel, seq=seq, scale=hd ** -0.5),
        out_shape=jax.ShapeDtypeStruct((t, d), BF16),
        grid=(n_seq, n_heads, nq),
        in_specs=[pl.BlockSpec(memory_space=pltpu.SMEM), qspec, kspec, kspec],
        out_specs=qspec,
        scratch_shapes=[pltpu.VMEM((seq, hd), BF16), pltpu.VMEM((seq, hd), BF16), pltpu.VMEM((LANES, hd), F32)],
        compiler_params=_params("arbitrary", "arbitrary", "arbitrary"),
        name="moba_prompt",
    )(_alibi_slopes(n_heads), q, k, v)


def _lambda_value(lam_ref, lam_init):
    l = lam_ref[...]
    a = jnp.exp(jnp.sum(l[0:1] * l[1:2], axis=-1, keepdims=True))
    b = jnp.exp(jnp.sum(l[2:3] * l[3:4], axis=-1, keepdims=True))
    return a - b + lam_init


def _sub_norm(o, gsub_ref, lam_init):
    y = o * lax.rsqrt(jnp.mean(o * o, axis=-1, keepdims=True) + RMS_EPS)
    return (y * gsub_ref[...]) * (1.0 - lam_init)


def _diff_prompt_kernel(slopes_ref, lam_ref, gsub_ref, q_ref, k_ref, v_ref, o_ref, kb_s, vb_s, *,
                        seq, tq, scale, lam_init):
    h = pl.program_id(1)
    qi = pl.program_id(2)
    hd = q_ref.shape[1] // 2

    @pl.when(qi == 0)
    def _():
        for b in range(seq // tq):
            sl = slice(b * tq, (b + 1) * tq)
            kb_s[sl, :] = k_ref[sl, :].astype(BF16)
            vb_s[sl, :] = v_ref[sl, :].astype(BF16)

    q = q_ref[...]
    slope = slopes_ref[h]
    rr = lax.broadcasted_iota(I32, (tq, tq), 0)
    cc = lax.broadcasted_iota(I32, (tq, tq), 1)
    d0 = (rr - cc).astype(F32)

    def block_update(j, carry, diagonal):
        off = pl.multiple_of(j * tq, tq)
        kj = kb_s[pl.ds(off, tq), :]
        vj = vb_s[pl.ds(off, tq), :]
        bias = slope * (d0 + ((qi - j) * tq).astype(F32))
        out = []
        for c in range(2):
            s = lax.dot_general(q[:, c * hd:(c + 1) * hd], kj[:, c * hd:(c + 1) * hd], _NT,
                                preferred_element_type=F32) * scale - bias
            if diagonal:
                s = jnp.where(d0 >= 0.0, s, NEG_INF)
            out.extend(_online_update(s, vj, *carry[3 * c:3 * c + 3]))
        return tuple(out)

    one = (jnp.full((tq, 1), NEG_INF, F32), jnp.zeros((tq, 1), F32), jnp.zeros((tq, 2 * hd), F32))
    carry = lax.fori_loop(0, qi, lambda j, c: block_update(j, c, False), one + one)
    _, l0, a0, _, l1, a1 = block_update(qi, carry, True)
    o = a0 / l0 - _lambda_value(lam_ref, lam_init) * (a1 / l1)
    o_ref[...] = _sub_norm(o, gsub_ref, lam_init).astype(o_ref.dtype)


def _diff_prompt(q, k, v, lam_vecs, g_sub, n_seq, n_heads, lam_init):
    t, d = q.shape
    seq = t // n_seq
    hw = d // n_heads
    tq = 256
    nq = seq // tq
    qspec = pl.BlockSpec((tq, hw), lambda n, h, i: (n * nq + i, h))
    kspec = pl.BlockSpec((seq, hw), lambda n, h, i: (n, h))
    return pl.pallas_call(
        functools.partial(_diff_prompt_kernel, seq=seq, tq=tq, scale=(hw // 2) ** -0.5, lam_init=lam_init),
        out_shape=jax.ShapeDtypeStruct((t, d), BF16),
        grid=(n_seq, n_heads, nq),
        in_specs=[pl.BlockSpec(memory_space=pltpu.SMEM),
                  pl.BlockSpec((4, hw // 2), lambda n, h, i: (0, 0)),
                  pl.BlockSpec((1, hw), lambda n, h, i: (0, 0)),
                  qspec, kspec, kspec],
        out_specs=qspec,
        scratch_shapes=[pltpu.VMEM((seq, hw), BF16), pltpu.VMEM((seq, hw), BF16)],
        compiler_params=_params("arbitrary", "arbitrary", "arbitrary"),
        name="diff_prompt",
    )(_alibi_slopes(n_heads), lam_vecs, g_sub.reshape(1, hw), q, k, v)


def _row_slopes(n_heads, row_heads):
    return _alibi_slopes(n_heads)[row_heads].reshape(-1, 1)


def _page_bias(slope, row_t, row_g, n_groups, cols, past):
    col = lax.broadcasted_iota(I32, (row_t.shape[0], cols), 1)
    dist = (past + row_t - col // n_groups).astype(F32)
    return jnp.where(col % n_groups == row_g, -slope * dist, NEG_INF)


def _new_key_logits(qb, kn, slope, row_t, row_g, n_groups, scale):
    col = lax.broadcasted_iota(I32, (qb.shape[0], kn.shape[0]), 1)
    t2 = col // n_groups
    s = lax.dot_general(qb, kn.astype(BF16), _NT, preferred_element_type=F32) * scale
    s = s - slope * (row_t - t2).astype(F32)
    ok = jnp.where(col % n_groups == row_g, t2, row_t + 1) <= row_t
    return jnp.where(ok, s, NEG_INF)


def _moba_sample_kernel(pt_ref, q_ref, kn_ref, vn_ref, kc_ref, vc_ref, slope_ref, o_ref,
                        bias_s, m_s, l_s, acc_s, sc_s, *, n_heads, n_new, past, page, scale):
    p = pl.program_id(1)
    n_pages = pl.num_programs(1)
    rows = n_new * n_heads
    hd = o_ref.shape[1]
    cols = page * n_heads
    row = lax.broadcasted_iota(I32, (rows, 1), 0)
    row_t = row // n_heads
    row_h = row % n_heads
    slope = slope_ref[...]

    @pl.when(p == 0)
    def _():
        bias_s[...] = _page_bias(slope, row_t, row_h, n_heads, cols, past)

    q = q_ref[...]
    qb = q.astype(BF16)
    k3 = kc_ref[...]
    s = lax.dot_general(qb, k3.reshape(cols, hd).astype(BF16), _NT, preferred_element_type=F32) * scale
    s = s + (bias_s[...] + slope * (p * page).astype(F32))
    m = jnp.max(s, axis=-1, keepdims=True)
    e = jnp.exp(s - m)
    vp = vc_ref[...].reshape(cols, hd).astype(BF16)
    m_s[p] = jnp.broadcast_to(m, (rows, LANES))
    l_s[p] = jnp.broadcast_to(jnp.sum(e, axis=-1, keepdims=True), (rows, LANES))
    acc_s[p] = jnp.dot(e.astype(BF16), vp, preferred_element_type=F32)
    ksum = jnp.sum(k3, axis=0)
    ksum = jnp.concatenate([ksum] * n_new, axis=0)
    sc_s[p] = jnp.broadcast_to(jnp.sum(q * ksum, axis=-1, keepdims=True), (rows, LANES))

    @pl.when(p == n_pages - 1)
    def _():
        ppb = MOBA_BLOCK // page
        lane_b = lax.broadcasted_iota(I32, (rows, LANES), 1)

        def add_score(pp, sc):
            return jnp.where(lane_b == pp // ppb, sc + sc_s[pp], sc)

        score = lax.fori_loop(0, n_pages, add_score, jnp.zeros((rows, LANES), F32)) * (1.0 / MOBA_BLOCK)
        score = jnp.where(lane_b < past // MOBA_BLOCK, score, NEG_INF)
        sel = _top_lanes(score, MOBA_TOPK)

        s_own = _new_key_logits(qb, kn_ref[...], slope, row_t, row_h, n_heads, scale)
        m_own = jnp.max(s_own, axis=-1, keepdims=True)

        def page_sel(pp):
            return _lane_column(sel, pp // ppb) > 0.5

        def max_body(pp, mx):
            return jnp.maximum(mx, jnp.where(page_sel(pp), m_s[pp], NEG_INF))

        m_fin = lax.fori_loop(0, n_pages, max_body, jnp.broadcast_to(m_own, (rows, LANES)))

        def acc_body(pp, carry):
            l_i, acc = carry
            w = jnp.where(page_sel(pp), jnp.exp(m_s[pp] - m_fin), 0.0)
            return l_i + w * l_s[pp], acc + w * acc_s[pp]

        e_own = jnp.exp(s_own - m_fin[:, 0:1])
        o_own = jnp.dot(e_own.astype(BF16), vn_ref[...].astype(BF16), preferred_element_type=F32)
        l0 = jnp.broadcast_to(jnp.sum(e_own, axis=-1, keepdims=True), (rows, LANES))
        l_fin, acc = lax.fori_loop(0, n_pages, acc_body, (l0, o_own))
        o_ref[...] = (acc / l_fin).astype(o_ref.dtype)


def _moba_sample(q, k_new, v_new, cache_k, cache_v, layer, page_table, n_heads):
    n_seq, n_pages = page_table.shape
    t, d = q.shape
    n_new = t // n_seq
    page = cache_k.shape[2]
    hd = d // n_heads
    rows = n_new * n_heads
    row_heads = jnp.arange(rows) % n_heads
    seq_spec = pl.BlockSpec((None, rows, hd), lambda b, p, pt: (b, 0, 0))
    page_spec = pl.BlockSpec((None, None, page, n_heads, hd), lambda b, p, pt: (layer, pt[b, p], 0, 0, 0))
    kernel = functools.partial(_moba_sample_kernel, n_heads=n_heads, n_new=n_new, past=n_pages * page,
                               page=page, scale=hd ** -0.5)
    out = pl.pallas_call(
        kernel,
        out_shape=jax.ShapeDtypeStruct((n_seq, rows, hd), BF16),
        grid_spec=pltpu.PrefetchScalarGridSpec(
            num_scalar_prefetch=1,
            grid=(n_seq, n_pages),
            in_specs=[seq_spec, seq_spec, seq_spec, page_spec, page_spec,
                      pl.BlockSpec((rows, 1), lambda b, p, pt: (0, 0))],
            out_specs=seq_spec,
            scratch_shapes=[pltpu.VMEM((rows, page * n_heads), F32)] + [pltpu.VMEM((n_pages, rows, LANES), F32)] * 4,
        ),
        compiler_params=_params("arbitrary", "arbitrary"),
        name="moba_sample",
    )(page_table, q.astype(F32).reshape(n_seq, rows, hd), k_new.reshape(n_seq, rows, hd),
      v_new.reshape(n_seq, rows, hd), cache_k, cache_v, _row_slopes(n_heads, row_heads))
    return out.reshape(t, d)


def _diff_sample_kernel(pt_ref, q_ref, kn_ref, vn_ref, kc_ref, vc_ref, slope_ref, lam_ref, gsub_ref, o_ref,
                        bias_s, m_s, l_s, acc_s, *, n_heads, n_new, past, page, scale, lam_init):
    p = pl.program_id(1)
    n_pages = pl.num_programs(1)
    half = n_new * n_heads
    rows = 2 * half
    hd = q_ref.shape[1]
    hw = o_ref.shape[1]
    cols = page * n_heads
    row = lax.broadcasted_iota(I32, (rows, 1), 0)
    row_t = (row % half) // n_heads
    row_h = row % n_heads
    slope = slope_ref[...]

    @pl.when(p == 0)
    def _():
        bias_s[...] = _page_bias(slope, row_t, row_h, n_heads, cols, past)
        m_s[...] = jnp.full(m_s.shape, NEG_INF, F32)
        l_s[...] = jnp.zeros(l_s.shape, F32)
        acc_s[...] = jnp.zeros(acc_s.shape, F32)

    qb = q_ref[...].astype(BF16)

    def update(s, values):
        m_prev = m_s[...]
        m_new = jnp.maximum(m_prev, jnp.max(s, axis=-1, keepdims=True))
        alpha = jnp.exp(m_prev - m_new)
        e = jnp.exp(s - m_new[:, 0:1])
        m_s[...] = m_new
        l_s[...] = alpha * l_s[...] + jnp.sum(e, axis=-1, keepdims=True)
        acc_s[...] = alpha[:, 0:1] * acc_s[...] + jnp.dot(e.astype(BF16), values, preferred_element_type=F32)

    s_parts = []
    for c in range(2):
        kc = kc_ref[:, pl.ds(c, n_heads, stride=2), :].reshape(cols, hd)
        s_parts.append(lax.dot_general(qb[c * half:(c + 1) * half], kc.astype(BF16), _NT,
                                       preferred_element_type=F32))
    s = jnp.concatenate(s_parts, axis=0) * scale + (bias_s[...] + slope * (p * page).astype(F32))
    update(s, vc_ref[...].reshape(cols, hw).astype(BF16))

    @pl.when(p == n_pages - 1)
    def _():
        s_new = jnp.concatenate(
            [_new_key_logits(qb[c * half:(c + 1) * half], kn_ref[c], slope[:half], row_t[:half], row_h[:half],
                             n_heads, scale) for c in range(2)], axis=0)
        update(s_new, vn_ref[...].astype(BF16))
        o_c = acc_s[...] / l_s[:, 0:1]
        o = o_c[:half] - _lambda_value(lam_ref, lam_init) * o_c[half:]
        o_ref[...] = _sub_norm(o, gsub_ref, lam_init).astype(o_ref.dtype)


def _diff_sample(q, k_new, v_new, cache_k, cache_v, layer, page_table, n_heads, lam_vecs, g_sub, lam_init):
    n_seq, n_pages = page_table.shape
    t, d = q.shape
    n_new = t // n_seq
    page = cache_k.shape[2]
    hw = d // n_heads
    hd = hw // 2
    half = n_new * n_heads
    rows = 2 * half
    row_heads = jnp.arange(rows) % n_heads

    def by_component(x):
        return x.reshape(n_seq, n_new, n_heads, 2, hd).transpose(0, 3, 1, 2, 4).reshape(n_seq, 2, half, hd)

    kpage_spec = pl.BlockSpec((None, None, page, 2 * n_heads, hd), lambda b, p, pt: (layer, pt[b, p], 0, 0, 0))
    vpage_spec = pl.BlockSpec((None, None, page, n_heads, hw), lambda b, p, pt: (layer, pt[b, p], 0, 0, 0))
    kernel = functools.partial(_diff_sample_kernel, n_heads=n_heads, n_new=n_new, past=n_pages * page,
                               page=page, scale=hd ** -0.5, lam_init=lam_init)
    out = pl.pallas_call(
        kernel,
        out_shape=jax.ShapeDtypeStruct((n_seq, half, hw), BF16),
        grid_spec=pltpu.PrefetchScalarGridSpec(
            num_scalar_prefetch=1,
            grid=(n_seq, n_pages),
            in_specs=[pl.BlockSpec((None, rows, hd), lambda b, p, pt: (b, 0, 0)),
                      pl.BlockSpec((None, 2, half, hd), lambda b, p, pt: (b, 0, 0, 0)),
                      pl.BlockSpec((None, half, hw), lambda b, p, pt: (b, 0, 0)),
                      kpage_spec, vpage_spec,
                      pl.BlockSpec((rows, 1), lambda b, p, pt: (0, 0)),
                      pl.BlockSpec((4, hd), lambda b, p, pt: (0, 0)),
                      pl.BlockSpec((1, hw), lambda b, p, pt: (0, 0))],
            out_specs=pl.BlockSpec((None, half, hw), lambda b, p, pt: (b, 0, 0)),
            scratch_shapes=[pltpu.VMEM((rows, page * n_heads), F32), pltpu.VMEM((rows, LANES), F32),
                            pltpu.VMEM((rows, LANES), F32), pltpu.VMEM((rows, hw), F32)],
        ),
        compiler_params=_params("arbitrary", "arbitrary"),
        name="diff_sample",
    )(page_table, by_component(q.astype(F32)).reshape(n_seq, rows, hd), by_component(k_new),
      v_new.reshape(n_seq, half, hw), cache_k, cache_v, _row_slopes(n_heads, row_heads), lam_vecs,
      g_sub.reshape(1, hw))
    return out.reshape(t, d)


def _first_index_of_max(vals, idx, n):
    m = jnp.max(vals, axis=0, keepdims=True)
    first = jnp.min(jnp.where(vals == m, idx, float(n)), axis=0, keepdims=True)
    return m, first


def _ffn_prologue_kernel(x_ref, g_ref, sh_ref, sc_ref, wr_ref, br_ref, h_ref, idx_ref, gw_ref):
    h = _modnorm_value(x_ref[...], g_ref[...], sh_ref[...], sc_ref[...])
    h_ref[...] = h
    logits = lax.dot_general(wr_ref[...], h, _NT, precision=lax.Precision.HIGHEST, preferred_element_type=F32)
    s = jax.nn.sigmoid(logits)
    choice = s + br_ref[...]
    n_exp, tm = s.shape
    per = n_exp // N_GROUPS
    sub = lax.broadcasted_iota(I32, (per, tm), 0).astype(F32)
    grp_rows = []
    for g in range(N_GROUPS):
        cg = choice[g * per:(g + 1) * per, :]
        m1, first = _first_index_of_max(cg, sub, per)
        m2 = jnp.max(jnp.where(sub == first, NEG_INF, cg), axis=0, keepdims=True)
        grp_rows.append(m1 + m2)
    grp = jnp.concatenate(grp_rows, axis=0)
    gi = lax.broadcasted_iota(I32, (N_GROUPS, tm), 0).astype(F32)
    keep = jnp.zeros((N_GROUPS, tm), F32)
    for _ in range(TOPK_GROUPS):
        _, first = _first_index_of_max(grp, gi, N_GROUPS)
        hit = gi == first
        keep = jnp.where(hit, 1.0, keep)
        grp = jnp.where(hit, NEG_INF, grp)
    masked = jnp.concatenate(
        [jnp.where(keep[g:g + 1, :] > 0.5, choice[g * per:(g + 1) * per, :], NEG_INF) for g in range(N_GROUPS)],
        axis=0)
    ei = lax.broadcasted_iota(I32, (n_exp, tm), 0).astype(F32)
    idx_rows, w_rows = [], []
    for _ in range(TOP_K):
        _, first = _first_index_of_max(masked, ei, n_exp)
        hit = ei == first
        idx_rows.append(first)
        w_rows.append(jnp.sum(jnp.where(hit, s, 0.0), axis=0, keepdims=True))
        masked = jnp.where(hit, NEG_INF, masked)
    w = jnp.concatenate(w_rows, axis=0)
    idx_ref[...] = jnp.concatenate(idx_rows, axis=0).astype(I32)
    gw_ref[...] = w / jnp.sum(w, axis=0, keepdims=True) * ROUTED_SCALE


def _ffn_prologue(x, g, mod, w_router_t, b_router, tm):
    t, d = x.shape
    n_exp = w_router_t.shape[0]
    return pl.pallas_call(
        _ffn_prologue_kernel,
        out_shape=(jax.ShapeDtypeStruct((t, d), F32), jax.ShapeDtypeStruct((TOP_K, t), I32),
                   jax.ShapeDtypeStruct((TOP_K, t), F32)),
        grid=(t // tm,),
        in_specs=[pl.BlockSpec((tm, d), lambda i: (i, 0)),
                  pl.BlockSpec((1, d), lambda i: (0, 0)),
                  mod.spec(0, tm, d, d, 0),
                  mod.spec(1, tm, d, d, 0),
                  pl.BlockSpec((n_exp, d), lambda i: (0, 0)),
                  pl.BlockSpec((n_exp, 1), lambda i: (0, 0))],
        out_specs=(pl.BlockSpec((tm, d), lambda i: (i, 0)),
                   pl.BlockSpec((TOP_K, tm), lambda i: (0, i)),
                   pl.BlockSpec((TOP_K, tm), lambda i: (0, i))),
        compiler_params=_params("arbitrary"),
        name="ffn_prologue",
    )(x, g.reshape(1, d), mod.arr, mod.arr, w_router_t, b_router.reshape(n_exp, 1))


def _swiglu(x, wg_s, wu_s, wd_s):
    gate = jnp.dot(x, wg_s[...], preferred_element_type=F32)
    up = jnp.dot(x, wu_s[...], preferred_element_type=F32)
    a = (gate * jax.nn.sigmoid(gate) * up).astype(BF16)
    return jnp.dot(a, wd_s[...], preferred_element_type=F32)


def _row_gather_copy(src_hbm, src_row, dst, dst_row, sem):
    return pltpu.make_async_copy(src_hbm.at[pl.ds(src_row, 1), :], dst.at[pl.ds(dst_row, 1), :], sem)


def _expert_kernel(be_ref, nv_ref, tok_ref, h_hbm, wg_ref, wu_ref, wd_ref, y_ref, wg_s, wu_s, wd_s, x_s, sem):
    i = pl.program_id(0)
    bm = y_ref.shape[0]
    n_live = nv_ref[0]
    live = i < n_live
    changed = jnp.logical_or(i == 0, be_ref[i] != be_ref[jnp.maximum(i - 1, 0)])

    def start_gather(blk, slot):
        def body(r, carry):
            _row_gather_copy(h_hbm, tok_ref[blk * bm + r], x_s.at[slot], r, sem.at[slot]).start()
            return carry

        lax.fori_loop(0, bm, body, 0, unroll=8)

    @pl.when(jnp.logical_and(i == 0, live))
    def _():
        start_gather(0, 0)

    @pl.when(i + 1 < n_live)
    def _():
        start_gather(i + 1, (i + 1) % 2)

    @pl.when(jnp.logical_and(live, changed))
    def _():
        wg_s[...] = wg_ref[...].astype(BF16)
        wu_s[...] = wu_ref[...].astype(BF16)
        wd_s[...] = wd_ref[...].astype(BF16)

    @pl.when(live)
    def _():
        slot = i % 2
        pltpu.make_async_copy(h_hbm.at[pl.ds(0, bm), :], x_s.at[slot], sem.at[slot]).wait()
        y_ref[...] = _swiglu(x_s[slot].astype(BF16), wg_s, wu_s, wd_s)

    @pl.when(jnp.logical_not(live))
    def _():
        y_ref[...] = jnp.zeros(y_ref.shape, y_ref.dtype)


def _experts(h, slot_tok, block_e, n_valid, wg, wu, wd, layer, bm):
    t, d = h.shape
    f = wg.shape[-1]
    n_blocks = slot_tok.shape[0] // bm
    up_spec = pl.BlockSpec((None, None, d, f), lambda i, be, nv, tok: (layer, be[i], 0, 0))
    down_spec = pl.BlockSpec((None, None, f, d), lambda i, be, nv, tok: (layer, be[i], 0, 0))
    return pl.pallas_call(
        _expert_kernel,
        out_shape=jax.ShapeDtypeStruct((n_blocks * bm, d), F32),
        grid_spec=pltpu.PrefetchScalarGridSpec(
            num_scalar_prefetch=3,
            grid=(n_blocks,),
            in_specs=[pl.BlockSpec(memory_space=pl.ANY), up_spec, up_spec, down_spec],
            out_specs=pl.BlockSpec((bm, d), lambda i, be, nv, tok: (i, 0)),
            scratch_shapes=[pltpu.VMEM((d, f), BF16), pltpu.VMEM((d, f), BF16), pltpu.VMEM((f, d), BF16),
                            pltpu.VMEM((2, bm, d), F32), pltpu.SemaphoreType.DMA((2,))],
        ),
        compiler_params=_params("arbitrary"),
        name="experts",
    )(block_e, n_valid, slot_tok, h, wg, wu, wd)


def _shared_kernel(x_ref, wg_ref, wu_ref, wd_ref, y_ref, wg_s, wu_s, wd_s):
    @pl.when(pl.program_id(0) == 0)
    def _():
        wg_s[...] = wg_ref[...].astype(BF16)
        wu_s[...] = wu_ref[...].astype(BF16)
        wd_s[...] = wd_ref[...].astype(BF16)

    y_ref[...] = _swiglu(x_ref[...].astype(BF16), wg_s, wu_s, wd_s)


def _shared_expert(h, wg, wu, wd, layer, tm):
    t, d = h.shape
    f = wg.shape[-1]
    return pl.pallas_call(
        _shared_kernel,
        out_shape=jax.ShapeDtypeStruct((t, d), F32),
        grid=(t // tm,),
        in_specs=[pl.BlockSpec((tm, d), lambda i: (i, 0)),
                  pl.BlockSpec((None, d, f), lambda i: (layer, 0, 0)),
                  pl.BlockSpec((None, d, f), lambda i: (layer, 0, 0)),
                  pl.BlockSpec((None, f, d), lambda i: (layer, 0, 0))],
        out_specs=pl.BlockSpec((tm, d), lambda i: (i, 0)),
        scratch_shapes=[pltpu.VMEM((d, f), BF16), pltpu.VMEM((d, f), BF16), pltpu.VMEM((f, d), BF16)],
        compiler_params=_params("arbitrary"),
        name="shared_expert",
    )(h, wg, wu, wd)


def _dispatch_plan(idx_t, n_exp, bm):
    k, t = idx_t.shape
    a = t * k
    flat_e = idx_t.T.reshape(a)
    onehot = (flat_e[:, None] == jnp.arange(n_exp, dtype=I32)[None, :]).astype(I32)
    seen = jnp.cumsum(onehot, axis=0)
    counts = seen[-1]
    padded = (counts + bm - 1) // bm * bm
    pad_end = jnp.cumsum(padded)
    pad_start = pad_end - padded
    pos = jnp.sum(onehot * (seen - 1 + pad_start[None, :]), axis=1).astype(I32)
    n_blocks = -(-a // bm) + n_exp
    slot_tok = jnp.zeros((n_blocks * bm,), I32).at[pos].set(jnp.arange(a, dtype=I32) // k)
    block_e = jnp.minimum(jnp.searchsorted(pad_end, jnp.arange(n_blocks) * bm, side='right'), n_exp - 1).astype(I32)
    n_valid = (pad_end[-1:] // bm).astype(I32)
    return slot_tok, block_e, n_valid, pos


def _ffn_out_kernel(pos_ref, x_ref, gate_ref, gw_ref, shared_ref, yb_hbm, o_ref, y_s, sem, *, row_off, top_k):
    i = pl.program_id(0)
    n = pl.num_programs(0)
    tm = x_ref.shape[0]

    def start_gather(tile, slot):
        base = (row_off + tile) * tm * top_k

        def body(r, carry):
            for k in range(top_k):
                _row_gather_copy(yb_hbm, pos_ref[base + r * top_k + k], y_s.at[slot, k], r, sem.at[slot]).start()
            return carry

        lax.fori_loop(0, tm, body, 0, unroll=2)

    @pl.when(i == 0)
    def _():
        start_gather(0, 0)

    @pl.when(i + 1 < n)
    def _():
        start_gather(i + 1, (i + 1) % 2)

    slot = i % 2
    for k in range(top_k):
        pltpu.make_async_copy(yb_hbm.at[pl.ds(0, tm), :], y_s.at[slot, k], sem.at[slot]).wait()
    gw = gw_ref[...]
    routed = gw[:, 0:1] * y_s[slot, 0]
    for k in range(1, top_k):
        routed = routed + gw[:, k:k + 1] * y_s[slot, k]
    o_ref[...] = x_ref[...] + gate_ref[...] * (routed + shared_ref[...])


def _ffn_out(x, mod, gw, pos, yb, shared, row_off, tm):
    t, d = x.shape
    top_k = gw.shape[1]
    return pl.pallas_call(
        functools.partial(_ffn_out_kernel, row_off=row_off, top_k=top_k),
        out_shape=jax.ShapeDtypeStruct((t, d), F32),
        grid_spec=pltpu.PrefetchScalarGridSpec(
            num_scalar_prefetch=1,
            grid=(t // tm,),
            in_specs=[pl.BlockSpec((tm, d), lambda i, pos: (i, 0)),
                      mod.spec(2, tm, d, d, 0),
                      pl.BlockSpec((tm, top_k), lambda i, pos: (row_off + i, 0)),
                      pl.BlockSpec((tm, d), lambda i, pos: (row_off + i, 0)),
                      pl.BlockSpec(memory_space=pl.ANY)],
            out_specs=pl.BlockSpec((tm, d), lambda i, pos: (i, 0)),
            scratch_shapes=[pltpu.VMEM((2, top_k, tm, d), F32), pltpu.SemaphoreType.DMA((2,))],
        ),
        compiler_params=_params("arbitrary"),
        name="ffn_out",
    )(pos, x, mod.arr, gw, shared, yb)


def kernel(x_prompt, x_sample, cache_k_moba, cache_v_moba, cache_k_diff, cache_v_diff, page_table, c_prompt, c_sample, g_mix, w_ada_mix, b_ada_mix, w_qkv_moba, g_q_moba, g_k_moba, w_o_moba, w_qkv_diff, g_q_diff, g_k_diff, lambda_q1, lambda_k1, lambda_q2, lambda_k2, g_sub_diff, w_o_diff, g_ffn, w_ada_ffn, b_ada_ffn, w_router, b_router, w_gate_exp, w_up_exp, w_down_exp, w_gate_sh, w_up_sh, w_down_sh):
    n_p, seq, d = x_prompt.shape
    n_s, n_new, _ = x_sample.shape
    depth = g_mix.shape[0]
    t_p, t_s = n_p * seq, n_s * n_new
    moba_heads = cache_k_moba.shape[3]
    diff_heads = cache_k_diff.shape[3]
    n_exp = w_router.shape[2]
    tm_p = 512
    bm = 256
    tm_out = 64

    xp = x_prompt.reshape(t_p, d)
    xs = x_sample.reshape(t_s, d)

    c_all = jnp.concatenate([c_prompt, c_sample], axis=0)
    n_c = c_all.shape[0]
    c_all = jnp.pad(c_all, ((0, -n_c % 8), (0, 0)))
    ada_mix = _ada(c_all, w_ada_mix, b_ada_mix)
    ada_ffn = _ada(c_all, w_ada_ffn, b_ada_ffn)

    def mods(ada, i):
        mp = _Mod(ada[i, :n_p].reshape(n_p, 1, 3 * d), seq)
        ms = _Mod(jnp.repeat(ada[i, n_p:n_p + n_s], n_new, axis=0), n_new)
        return mp, ms

    ck_d = cache_k_diff.reshape(cache_k_diff.shape[:3] + (2 * diff_heads, d // (2 * diff_heads)))

    kv = {name: [] for name in ("k_mp", "v_mp", "k_dp", "v_dp", "k_ms", "v_ms", "k_ds", "v_ds")}
    for i in range(depth):
        j = i // 2
        mod_p, mod_s = mods(ada_mix, i)
        hp = _modnorm(xp, g_mix[i], mod_p, tm_p)
        hs = _modnorm(xs, g_mix[i], mod_s, t_s)
        if i % 2 == 0:
            gq = jnp.tile(g_q_moba[j], moba_heads)
            gk = jnp.tile(g_k_moba[j], moba_heads)
            qp, kp, vp = _qkv(hp, w_qkv_moba, j, gq, gk, tm_p)
            qs, ks, vs = _qkv(hs, w_qkv_moba, j, gq, gk, t_s)
            op = _moba_prompt(qp, kp, vp, n_p, moba_heads)
            os_ = _moba_sample(qs, ks, vs, cache_k_moba, cache_v_moba, j, page_table, moba_heads)
            w_o = w_o_moba
            kv["k_mp"].append(kp.reshape(n_p, seq, moba_heads, d // moba_heads))
            kv["v_mp"].append(vp.reshape(n_p, seq, moba_heads, d // moba_heads))
            kv["k_ms"].append(ks.reshape(n_s, n_new, moba_heads, d // moba_heads))
            kv["v_ms"].append(vs.reshape(n_s, n_new, moba_heads, d // moba_heads))
        else:
            lam_init = 0.8 - 0.6 * math.exp(-0.3 * i)
            lam_vecs = jnp.stack([lambda_q1[j], lambda_k1[j], lambda_q2[j], lambda_k2[j]]).astype(F32)
            gq = jnp.tile(g_q_diff[j].reshape(-1), diff_heads)
            gk = jnp.tile(g_k_diff[j].reshape(-1), diff_heads)
            qp, kp, vp = _qkv(hp, w_qkv_diff, j, gq, gk, tm_p)
            qs, ks, vs = _qkv(hs, w_qkv_diff, j, gq, gk, t_s)
            op = _diff_prompt(qp, kp, vp, lam_vecs, g_sub_diff[j], n_p, diff_heads, lam_init)
            os_ = _diff_sample(qs, ks, vs, ck_d, cache_v_diff, j, page_table, diff_heads, lam_vecs,
                               g_sub_diff[j], lam_init)
            w_o = w_o_diff
            hw = d // diff_heads
            kv["k_dp"].append(kp.reshape(n_p, seq, diff_heads, 2, hw // 2))
            kv["v_dp"].append(vp.reshape(n_p, seq, diff_heads, hw))
            kv["k_ds"].append(ks.reshape(n_s, n_new, diff_heads, 2, hw // 2))
            kv["v_ds"].append(vs.reshape(n_s, n_new, diff_heads, hw))
        xp = _oproj(op, w_o, j, xp, mod_p, tm_p)
        xs = _oproj(os_, w_o, j, xs, mod_s, t_s)

        mod_p, mod_s = mods(ada_ffn, i)
        w_rt = w_router[i].T
        hp, idx_p, gw_p = _ffn_prologue(xp, g_ffn[i], mod_p, w_rt, b_router[i], 256)
        hs, idx_s, gw_s = _ffn_prologue(xs, g_ffn[i], mod_s, w_rt, b_router[i], t_s)
        h_all = jnp.concatenate([hp, hs], axis=0)
        idx_t = jnp.concatenate([idx_p, idx_s], axis=1)
        gw = jnp.concatenate([gw_p, gw_s], axis=1).T
        slot_tok, block_e, n_valid, pos = _dispatch_plan(idx_t, n_exp, bm)
        yb = _experts(h_all, slot_tok, block_e, n_valid, w_gate_exp, w_up_exp, w_down_exp, i, bm)
        shared = _shared_expert(h_all, w_gate_sh, w_up_sh, w_down_sh, i, t_s)
        xp = _ffn_out(xp, mod_p, gw, pos, yb, shared, 0, tm_out)
        xs = _ffn_out(xs, mod_s, gw, pos, yb, shared, t_p // tm_out, tm_out)

    return (xp.reshape(n_p, seq, d), xs.reshape(n_s, n_new, d),
            jnp.stack(kv["k_mp"]), jnp.stack(kv["v_mp"]), jnp.stack(kv["k_dp"]), jnp.stack(kv["v_dp"]),
            jnp.stack(kv["k_ms"]), jnp.stack(kv["v_ms"]), jnp.stack(kv["k_ds"]), jnp.stack(kv["v_ds"]))
```

```python
import functools
import math

import jax
import jax.numpy as jnp
from jax import lax
from jax.experimental import pallas as pl
from jax.experimental.pallas import tpu as pltpu

F32 = jnp.float32
BF16 = jnp.bfloat16
I32 = jnp.int32

MOBA_BLOCK = 256
MOBA_TOPK = 3
N_GROUPS = 8
TOPK_GROUPS = 4
TOP_K = 8
ROUTED_SCALE = 2.5
RMS_EPS = 1e-6
NEG_INF = -1e30

LANES = 128
VMEM_LIMIT = 48 * 1024 * 1024

_NT = (((1,), (1,)), ((), ()))


def _params(*sem):
    return pltpu.CompilerParams(dimension_semantics=sem, vmem_limit_bytes=VMEM_LIMIT)


def _alibi_slopes(n_heads):
    return jnp.asarray([2.0 ** (-8.0 * (h + 1) / n_heads) for h in range(n_heads)], dtype=F32)


class _Mod:
    def __init__(self, arr, rows_per_seq):
        self.arr = arr
        self.rows_per_seq = rows_per_seq
        self.per_token = arr.ndim == 2

    def spec(self, part, tm, d, tn, row_axis, col_axis=None):
        ncol = d // tn

        def col(g):
            return part * ncol + (g[col_axis] if col_axis is not None else 0)

        if self.per_token:
            return pl.BlockSpec((tm, tn), lambda *g: (g[row_axis], col(g)))
        tps = self.rows_per_seq // tm
        return pl.BlockSpec((None, 1, tn), lambda *g: (g[row_axis] // tps, 0, col(g)))


def _ada_kernel(c_ref, w_ref, b_ref, o_ref):
    c = c_ref[...]
    a = (c * jax.nn.sigmoid(c)).astype(BF16)
    o_ref[...] = jnp.dot(a, w_ref[...].astype(BF16), preferred_element_type=F32) + b_ref[...]


def _ada(c_all, w, b):
    depth, d, n3 = w.shape
    mp = c_all.shape[0]
    tn = 512
    return pl.pallas_call(
        _ada_kernel,
        out_shape=jax.ShapeDtypeStruct((depth, mp, n3), F32),
        grid=(depth, n3 // tn),
        in_specs=[pl.BlockSpec((mp, d), lambda l, j: (0, 0)),
                  pl.BlockSpec((None, d, tn), lambda l, j: (l, 0, j)),
                  pl.BlockSpec((None, 1, tn), lambda l, j: (l, 0, j))],
        out_specs=pl.BlockSpec((None, mp, tn), lambda l, j: (l, 0, j)),
        compiler_params=_params("arbitrary", "arbitrary"),
        name="ada",
    )(c_all, w, b.reshape(depth, 1, n3))


def _modnorm_value(x, g, shift, scale):
    y = x * lax.rsqrt(jnp.mean(x * x, axis=-1, keepdims=True) + RMS_EPS)
    return (y * g) * (1.0 + scale) + shift


def _modnorm_kernel(x_ref, g_ref, sh_ref, sc_ref, h_ref):
    h_ref[...] = _modnorm_value(x_ref[...], g_ref[...], sh_ref[...], sc_ref[...]).astype(h_ref.dtype)


def _modnorm(x, g, mod, tm):
    t, d = x.shape
    return pl.pallas_call(
        _modnorm_kernel,
        out_shape=jax.ShapeDtypeStruct((t, d), BF16),
        grid=(t // tm,),
        in_specs=[pl.BlockSpec((tm, d), lambda i: (i, 0)),
                  pl.BlockSpec((1, d), lambda i: (0, 0)),
                  mod.spec(0, tm, d, d, 0),
                  mod.spec(1, tm, d, d, 0)],
        out_specs=pl.BlockSpec((tm, d), lambda i: (i, 0)),
        compiler_params=_params("arbitrary"),
        name="modnorm",
    )(x, g.reshape(1, d), mod.arr, mod.arr)


def _group_norm_store(y, g_ref, o_ref):
    for c in range(y.shape[1] // LANES):
        sl = slice(c * LANES, (c + 1) * LANES)
        yc = y[:, sl]
        yc = yc * lax.rsqrt(jnp.mean(yc * yc, axis=-1, keepdims=True) + RMS_EPS) * g_ref[:, sl]
        o_ref[:, sl] = yc.astype(o_ref.dtype)


def _qkv_kernel(h_ref, wq_ref, wk_ref, wv_ref, gq_ref, gk_ref, q_ref, k_ref, v_ref, wq_s, wk_s, wv_s):
    @pl.when(pl.program_id(1) == 0)
    def _():
        wq_s[...] = wq_ref[...].astype(BF16)
        wk_s[...] = wk_ref[...].astype(BF16)
        wv_s[...] = wv_ref[...].astype(BF16)

    h = h_ref[...]
    _group_norm_store(jnp.dot(h, wq_s[...], preferred_element_type=F32), gq_ref, q_ref)
    _group_norm_store(jnp.dot(h, wk_s[...], preferred_element_type=F32), gk_ref, k_ref)
    v_ref[...] = jnp.dot(h, wv_s[...], preferred_element_type=F32)


def _qkv(h, w, layer, gq, gk, tm):
    t, d = h.shape
    tn = 256
    nj = d // tn
    wspec = lambda part: pl.BlockSpec((None, d, tn), lambda j, i: (layer, 0, part * nj + j))
    gspec = pl.BlockSpec((1, tn), lambda j, i: (0, j))
    ospec = pl.BlockSpec((tm, tn), lambda j, i: (i, j))
    return pl.pallas_call(
        _qkv_kernel,
        out_shape=(jax.ShapeDtypeStruct((t, d), BF16), jax.ShapeDtypeStruct((t, d), F32),
                   jax.ShapeDtypeStruct((t, d), F32)),
        grid=(nj, t // tm),
        in_specs=[pl.BlockSpec((tm, d), lambda j, i: (i, 0)), wspec(0), wspec(1), wspec(2), gspec, gspec],
        out_specs=(ospec, ospec, ospec),
        scratch_shapes=[pltpu.VMEM((d, tn), BF16)] * 3,
        compiler_params=_params("arbitrary", "arbitrary"),
        name="qkv",
    )(h, w, w, w, gq.reshape(1, d), gk.reshape(1, d))


def _oproj_kernel(o_ref, w_ref, x_ref, gate_ref, out_ref, w_s):
    @pl.when(pl.program_id(1) == 0)
    def _():
        w_s[...] = w_ref[...].astype(BF16)

    out_ref[...] = x_ref[...] + gate_ref[...] * jnp.dot(o_ref[...], w_s[...], preferred_element_type=F32)


def _oproj(o, w, layer, x, mod, tm):
    t, d = o.shape
    tn = 512
    return pl.pallas_call(
        _oproj_kernel,
        out_shape=jax.ShapeDtypeStruct((t, d), F32),
        grid=(d // tn, t // tm),
        in_specs=[pl.BlockSpec((tm, d), lambda j, i: (i, 0)),
                  pl.BlockSpec((None, d, tn), lambda j, i: (layer, 0, j)),
                  pl.BlockSpec((tm, tn), lambda j, i: (i, j)),
                  mod.spec(2, tm, d, tn, 1, 0)],
        out_specs=pl.BlockSpec((tm, tn), lambda j, i: (i, j)),
        scratch_shapes=[pltpu.VMEM((d, tn), BF16)],
        compiler_params=_params("arbitrary", "arbitrary"),
        name="oproj",
    )(o, w, x, mod.arr)


def _top_lanes(score, n_pick):
    lane = lax.broadcasted_iota(I32, score.shape, 1).astype(F32)
    sel = jnp.zeros(score.shape, F32)
    for _ in range(n_pick):
        m = jnp.max(score, axis=-1, keepdims=True)
        first = jnp.min(jnp.where(score == m, lane, float(LANES)), axis=-1, keepdims=True)
        hit = lane == first
        sel = jnp.where(hit & (m > 0.5 * NEG_INF), 1.0, sel)
        score = jnp.where(hit, NEG_INF, score)
    return sel


def _lane_column(mat, j):
    lane = lax.broadcasted_iota(I32, mat.shape, 1)
    return jnp.sum(jnp.where(lane == j, mat, 0.0), axis=-1, keepdims=True)


def _online_update(s, v, m_i, l_i, acc):
    m_new = jnp.maximum(m_i, jnp.max(s, axis=-1, keepdims=True))
    alpha = jnp.exp(m_i - m_new)
    p = jnp.exp(s - m_new)
    l_new = alpha * l_i + jnp.sum(p, axis=-1, keepdims=True)
    acc_new = alpha * acc + jnp.dot(p.astype(BF16), v, preferred_element_type=F32)
    return m_new, l_new, acc_new


def _moba_prompt_kernel(slopes_ref, q_ref, k_ref, v_ref, o_ref, kb_s, vb_s, km_s, *, seq, scale):
    h = pl.program_id(1)
    qi = pl.program_id(2)
    nb = seq // MOBA_BLOCK
    tq = MOBA_BLOCK

    @pl.when(qi == 0)
    def _():
        km_s[...] = jnp.zeros(km_s.shape, F32)
        for b in range(nb):
            sl = slice(b * MOBA_BLOCK, (b + 1) * MOBA_BLOCK)
            kf = k_ref[sl, :]
            kb_s[sl, :] = kf.astype(BF16)
            vb_s[sl, :] = v_ref[sl, :].astype(BF16)
            km_s[b:b + 1, :] = jnp.mean(kf, axis=0, keepdims=True)

    q = q_ref[...]
    score = lax.dot_general(q.astype(F32), km_s[...], _NT, precision=lax.Precision.HIGHEST,
                            preferred_element_type=F32)
    lane = lax.broadcasted_iota(I32, score.shape, 1)
    score = jnp.where(lane < qi, score, NEG_INF)
    sel = _top_lanes(score, MOBA_TOPK)

    slope = slopes_ref[h]
    n_chain = 1
    tr = tq // n_chain
    rr = lax.broadcasted_iota(I32, (tr, tq), 0)
    cc = lax.broadcasted_iota(I32, (tr, tq), 1)
    d0 = [(rr + c * tr - cc).astype(F32) for c in range(n_chain)]
    qc = [q[c * tr:(c + 1) * tr] for c in range(n_chain)]
    selc = [sel[c * tr:(c + 1) * tr] for c in range(n_chain)]

    def block_logits(j, c):
        off = pl.multiple_of(j * MOBA_BLOCK, MOBA_BLOCK)
        kj = kb_s[pl.ds(off, MOBA_BLOCK), :]
        vj = vb_s[pl.ds(off, MOBA_BLOCK), :]
        s = lax.dot_general(qc[c], kj, _NT, preferred_element_type=F32) * scale
        s = s - slope * (d0[c] + ((qi - j) * MOBA_BLOCK).astype(F32))
        return s, vj

    def past_block(j, carry):
        out = []
        for c in range(n_chain):
            s, vj = block_logits(j, c)
            s = jnp.where(_lane_column(selc[c], j) > 0.5, s, NEG_INF)
            out.extend(_online_update(s, vj, *carry[3 * c:3 * c + 3]))
        return tuple(out)

    one = (jnp.full((tr, 1), NEG_INF, F32), jnp.zeros((tr, 1), F32), jnp.zeros((tr, LANES), F32))
    carry = lax.fori_loop(0, qi, past_block, one * n_chain)
    for c in range(n_chain):
        s, vj = block_logits(qi, c)
        s = jnp.where(d0[c] >= 0.0, s, NEG_INF)
        _, l_i, acc = _online_update(s, vj, *carry[3 * c:3 * c + 3])
        o_ref[c * tr:(c + 1) * tr, :] = (acc / l_i).astype(o_ref.dtype)


def _moba_prompt(q, k, v, n_seq, n_heads):
    t, d = q.shape
    seq = t // n_seq
    hd = d // n_heads
    nq = seq // MOBA_BLOCK
    qspec = pl.BlockSpec((MOBA_BLOCK, hd), lambda n, h, i: (n * nq + i, h))
    kspec = pl.BlockSpec((seq, hd), lambda n, h, i: (n, h))
    return pl.pallas_call(
        functools.partial(_moba_prompt_kernel, seq=seq, scale=hd ** -0.5),
        out_shape=jax.ShapeDtypeStruct((t, d), BF16),
        grid=(n_seq, n_heads, nq),
        in_specs=[pl.BlockSpec(memory_space=pltpu.SMEM), qspec, kspec, kspec],
        out_specs=qspec,
        scratch_shapes=[pltpu.VMEM((seq, hd), BF16), pltpu.VMEM((seq, hd), BF16), pltpu.VMEM((LANES, hd), F32)],
        compiler_params=_params("arbitrary", "arbitrary", "arbitrary"),
        name="moba_prompt",
    )(_alibi_slopes(n_heads), q, k, v)


def _lambda_value(lam_ref, lam_init):
    l = lam_ref[...]
    a = jnp.exp(jnp.sum(l[0:1] * l[1:2], axis=-1, keepdims=True))
    b = jnp.exp(jnp.sum(l[2:3] * l[3:4], axis=-1, keepdims=True))
    return a - b + lam_init


def _sub_norm(o, gsub_ref, lam_init):
    y = o * lax.rsqrt(jnp.mean(o * o, axis=-1, keepdims=True) + RMS_EPS)
    return (y * gsub_ref[...]) * (1.0 - lam_init)


def _diff_prompt_kernel(slopes_ref, lam_ref, gsub_ref, q_ref, k_ref, v_ref, o_ref, kb_s, vb_s, *,
                        seq, tq, scale, lam_init):
    h = pl.program_id(1)
    qi = pl.program_id(2)
    hd = q_ref.shape[1] // 2

    @pl.when(qi == 0)
    def _():
        for b in range(seq // tq):
            sl = slice(b * tq, (b + 1) * tq)
            kb_s[sl, :] = k_ref[sl, :].astype(BF16)
            vb_s[sl, :] = v_ref[sl, :].astype(BF16)

    q = q_ref[...]
    slope = slopes_ref[h]
    rr = lax.broadcasted_iota(I32, (tq, tq), 0)
    cc = lax.broadcasted_iota(I32, (tq, tq), 1)
    d0 = (rr - cc).astype(F32)

    def block_update(j, carry, diagonal):
        off = pl.multiple_of(j * tq, tq)
        kj = kb_s[pl.ds(off, tq), :]
        vj = vb_s[pl.ds(off, tq), :]
        bias = slope * (d0 + ((qi - j) * tq).astype(F32))
        out = []
        for c in range(2):
            s = lax.dot_general(q[:, c * hd:(c + 1) * hd], kj[:, c * hd:(c + 1) * hd], _NT,
                                preferred_element_type=F32) * scale - bias
            if diagonal:
                s = jnp.where(d0 >= 0.0, s, NEG_INF)
            out.extend(_online_update(s, vj, *carry[3 * c:3 * c + 3]))
        return tuple(out)

    one = (jnp.full((tq, 1), NEG_INF, F32), jnp.zeros((tq, 1), F32), jnp.zeros((tq, 2 * hd), F32))
    carry = lax.fori_loop(0, qi, lambda j, c: block_update(j, c, False), one + one)
    _, l0, a0, _, l1, a1 = block_update(qi, carry, True)
    o = a0 / l0 - _lambda_value(lam_ref, lam_init) * (a1 / l1)
    o_ref[...] = _sub_norm(o, gsub_ref, lam_init).astype(o_ref.dtype)


def _diff_prompt(q, k, v, lam_vecs, g_sub, n_seq, n_heads, lam_init):
    t, d = q.shape
    seq = t // n_seq
    hw = d // n_heads
    tq = 256
    nq = seq // tq
    qspec = pl.BlockSpec((tq, hw), lambda n, h, i: (n * nq + i, h))
    kspec = pl.BlockSpec((seq, hw), lambda n, h, i: (n, h))
    return pl.pallas_call(
        functools.partial(_diff_prompt_kernel, seq=seq, tq=tq, scale=(hw // 2) ** -0.5, lam_init=lam_init),
        out_shape=jax.ShapeDtypeStruct((t, d), BF16),
        grid=(n_seq, n_heads, nq),
        in_specs=[pl.BlockSpec(memory_space=pltpu.SMEM),
                  pl.BlockSpec((4, hw // 2), lambda n, h, i: (0, 0)),
                  pl.BlockSpec((1, hw), lambda n, h, i: (0, 0)),
                  qspec, kspec, kspec],
        out_specs=qspec,
        scratch_shapes=[pltpu.VMEM((seq, hw), BF16), pltpu.VMEM((seq, hw), BF16)],
        compiler_params=_params("arbitrary", "arbitrary", "arbitrary"),
        name="diff_prompt",
    )(_alibi_slopes(n_heads), lam_vecs, g_sub.reshape(1, hw), q, k, v)


def _row_slopes(n_heads, row_heads):
    return _alibi_slopes(n_heads)[row_heads].reshape(-1, 1)


def _page_bias(slope, row_t, row_g, n_groups, cols, past):
    col = lax.broadcasted_iota(I32, (row_t.shape[0], cols), 1)
    dist = (past + row_t - col // n_groups).astype(F32)
    return jnp.where(col % n_groups == row_g, -slope * dist, NEG_INF)


def _new_key_logits(qb, kn, slope, row_t, row_g, n_groups, scale):
    col = lax.broadcasted_iota(I32, (qb.shape[0], kn.shape[0]), 1)
    t2 = col // n_groups
    s = lax.dot_general(qb, kn.astype(BF16), _NT, preferred_element_type=F32) * scale
    s = s - slope * (row_t - t2).astype(F32)
    ok = jnp.where(col % n_groups == row_g, t2, row_t + 1) <= row_t
    return jnp.where(ok, s, NEG_INF)


def _moba_sample_kernel(pt_ref, q_ref, kn_ref, vn_ref, kc0_ref, kc1_ref, vc0_ref, vc1_ref, slope_ref, o_ref,
                        bias_s, m_s, l_s, acc_s, sc_s, *, n_heads, n_new, past, page, scale, n_pages):
    step = pl.program_id(1)
    n_steps = pl.num_programs(1)
    rows = n_new * n_heads
    hd = o_ref.shape[1]
    cols = page * n_heads
    row = lax.broadcasted_iota(I32, (rows, 1), 0)
    row_t = row // n_heads
    row_h = row % n_heads
    slope = slope_ref[...]

    @pl.when(step == 0)
    def _():
        bias_s[...] = _page_bias(slope, row_t, row_h, n_heads, cols, past)

    q = q_ref[...]
    qb = q.astype(BF16)
    for u, (kc_ref, vc_ref) in enumerate(((kc0_ref, vc0_ref), (kc1_ref, vc1_ref))):
        p = 2 * step + u
        k3 = kc_ref[...]
        s = lax.dot_general(qb, k3.reshape(cols, hd).astype(BF16), _NT, preferred_element_type=F32) * scale
        s = s + (bias_s[...] + slope * (p * page).astype(F32))
        m = jnp.max(s, axis=-1, keepdims=True)
        e = jnp.exp(s - m)
        vp = vc_ref[...].reshape(cols, hd).astype(BF16)
        m_s[p] = jnp.broadcast_to(m, (rows, LANES))
        l_s[p] = jnp.broadcast_to(jnp.sum(e, axis=-1, keepdims=True), (rows, LANES))
        acc_s[p] = jnp.dot(e.astype(BF16), vp, preferred_element_type=F32)
        ksum = jnp.sum(k3, axis=0)
        ksum = jnp.concatenate([ksum] * n_new, axis=0)
        sc_s[p] = jnp.broadcast_to(jnp.sum(q * ksum, axis=-1, keepdims=True), (rows, LANES))

    @pl.when(step == n_steps - 1)
    def _():
        ppb = MOBA_BLOCK // page
        lane_b = lax.broadcasted_iota(I32, (rows, LANES), 1)

        def add_score(pp, sc):
            return jnp.where(lane_b == pp // ppb, sc + sc_s[pp], sc)

        score = lax.fori_loop(0, n_pages, add_score, jnp.zeros((rows, LANES), F32)) * (1.0 / MOBA_BLOCK)
        score = jnp.where(lane_b < past // MOBA_BLOCK, score, NEG_INF)
        sel = _top_lanes(score, MOBA_TOPK)

        s_own = _new_key_logits(qb, kn_ref[...], slope, row_t, row_h, n_heads, scale)
        m_own = jnp.max(s_own, axis=-1, keepdims=True)

        def page_sel(pp):
            return _lane_column(sel, pp // ppb) > 0.5

        def max_body(pp, mx):
            return jnp.maximum(mx, jnp.where(page_sel(pp), m_s[pp], NEG_INF))

        m_fin = lax.fori_loop(0, n_pages, max_body, jnp.broadcast_to(m_own, (rows, LANES)))

        def acc_body(pp, carry):
            l_i, acc = carry
            w = jnp.where(page_sel(pp), jnp.exp(m_s[pp] - m_fin), 0.0)
            return l_i + w * l_s[pp], acc + w * acc_s[pp]

        e_own = jnp.exp(s_own - m_fin[:, 0:1])
        o_own = jnp.dot(e_own.astype(BF16), vn_ref[...].astype(BF16), preferred_element_type=F32)
        l0 = jnp.broadcast_to(jnp.sum(e_own, axis=-1, keepdims=True), (rows, LANES))
        l_fin, acc = lax.fori_loop(0, n_pages, acc_body, (l0, o_own))
        o_ref[...] = (acc / l_fin).astype(o_ref.dtype)


def _moba_sample(q, k_new, v_new, cache_k, cache_v, layer, page_table, n_heads):
    n_seq, n_pages = page_table.shape
    t, d = q.shape
    n_new = t // n_seq
    page = cache_k.shape[2]
    hd = d // n_heads
    rows = n_new * n_heads
    row_heads = jnp.arange(rows) % n_heads
    seq_spec = pl.BlockSpec((None, rows, hd), lambda b, p, pt: (b, 0, 0))
    page_spec = lambda u: pl.BlockSpec((None, None, page, n_heads, hd),
                                       lambda b, p, pt: (layer, pt[b, 2 * p + u], 0, 0, 0))
    kernel = functools.partial(_moba_sample_kernel, n_heads=n_heads, n_new=n_new, past=n_pages * page,
                               page=page, scale=hd ** -0.5, n_pages=n_pages)
    out = pl.pallas_call(
        kernel,
        out_shape=jax.ShapeDtypeStruct((n_seq, rows, hd), BF16),
        grid_spec=pltpu.PrefetchScalarGridSpec(
            num_scalar_prefetch=1,
            grid=(n_seq, n_pages // 2),
            in_specs=[seq_spec, seq_spec, seq_spec, page_spec(0), page_spec(1), page_spec(0), page_spec(1),
                      pl.BlockSpec((rows, 1), lambda b, p, pt: (0, 0))],
            out_specs=seq_spec,
            scratch_shapes=[pltpu.VMEM((rows, page * n_heads), F32)] + [pltpu.VMEM((n_pages, rows, LANES), F32)] * 4,
        ),
        compiler_params=_params("arbitrary", "arbitrary"),
        name="moba_sample",
    )(page_table, q.astype(F32).reshape(n_seq, rows, hd), k_new.reshape(n_seq, rows, hd),
      v_new.reshape(n_seq, rows, hd), cache_k, cache_k, cache_v, cache_v, _row_slopes(n_heads, row_heads))
    return out.reshape(t, d)


def _diff_sample_kernel(pt_ref, q_ref, kn_ref, vn_ref, kc0_ref, kc1_ref, vc0_ref, vc1_ref, slope_ref, lam_ref,
                        gsub_ref, o_ref, bias_s, m_s, l_s, acc_s, *, n_heads, n_new, past, page, scale, lam_init):
    step = pl.program_id(1)
    n_steps = pl.num_programs(1)
    half = n_new * n_heads
    rows = 2 * half
    hd = q_ref.shape[1]
    hw = o_ref.shape[1]
    cols = page * n_heads
    row = lax.broadcasted_iota(I32, (rows, 1), 0)
    row_t = (row % half) // n_heads
    row_h = row % n_heads
    slope = slope_ref[...]

    @pl.when(step == 0)
    def _():
        bias_s[...] = _page_bias(slope, row_t, row_h, n_heads, cols, past)
        m_s[...] = jnp.full(m_s.shape, NEG_INF, F32)
        l_s[...] = jnp.zeros(l_s.shape, F32)
        acc_s[...] = jnp.zeros(acc_s.shape, F32)

    qb = q_ref[...].astype(BF16)

    def update(s, values):
        m_prev = m_s[...]
        m_new = jnp.maximum(m_prev, jnp.max(s, axis=-1, keepdims=True))
        alpha = jnp.exp(m_prev - m_new)
        e = jnp.exp(s - m_new[:, 0:1])
        m_s[...] = m_new
        l_s[...] = alpha * l_s[...] + jnp.sum(e, axis=-1, keepdims=True)
        acc_s[...] = alpha[:, 0:1] * acc_s[...] + jnp.dot(e.astype(BF16), values, preferred_element_type=F32)

    s_pages, v_pages = [], []
    for u, (kc_ref, vc_ref) in enumerate(((kc0_ref, vc0_ref), (kc1_ref, vc1_ref))):
        s_parts = []
        for c in range(2):
            kc = kc_ref[:, pl.ds(c, n_heads, stride=2), :].reshape(cols, hd)
            s_parts.append(lax.dot_general(qb[c * half:(c + 1) * half], kc.astype(BF16), _NT,
                                           preferred_element_type=F32))
        first_pos = ((2 * step + u) * page).astype(F32)
        s_pages.append(jnp.concatenate(s_parts, axis=0) * scale + (bias_s[...] + slope * first_pos))
        v_pages.append(vc_ref[...].reshape(cols, hw).astype(BF16))
    update(jnp.concatenate(s_pages, axis=1), jnp.concatenate(v_pages, axis=0))

    @pl.when(step == n_steps - 1)
    def _():
        s_new = jnp.concatenate(
            [_new_key_logits(qb[c * half:(c + 1) * half], kn_ref[c], slope[:half], row_t[:half], row_h[:half],
                             n_heads, scale) for c in range(2)], axis=0)
        update(s_new, vn_ref[...].astype(BF16))
        o_c = acc_s[...] / l_s[:, 0:1]
        o = o_c[:half] - _lambda_value(lam_ref, lam_init) * o_c[half:]
        o_ref[...] = _sub_norm(o, gsub_ref, lam_init).astype(o_ref.dtype)


def _diff_sample(q, k_new, v_new, cache_k, cache_v, layer, page_table, n_heads, lam_vecs, g_sub, lam_init):
    n_seq, n_pages = page_table.shape
    t, d = q.shape
    n_new = t // n_seq
    page = cache_k.shape[2]
    hw = d // n_heads
    hd = hw // 2
    half = n_new * n_heads
    rows = 2 * half
    row_heads = jnp.arange(rows) % n_heads

    def by_component(x):
        return x.reshape(n_seq, n_new, n_heads, 2, hd).transpose(0, 3, 1, 2, 4).reshape(n_seq, 2, half, hd)

    kpage_spec = lambda u: pl.BlockSpec((None, None, page, 2 * n_heads, hd),
                                        lambda b, p, pt: (layer, pt[b, 2 * p + u], 0, 0, 0))
    vpage_spec = lambda u: pl.BlockSpec((None, None, page, n_heads, hw),
                                        lambda b, p, pt: (layer, pt[b, 2 * p + u], 0, 0, 0))
    kernel = functools.partial(_diff_sample_kernel, n_heads=n_heads, n_new=n_new, past=n_pages * page,
                               page=page, scale=hd ** -0.5, lam_init=lam_init)
    out = pl.pallas_call(
        kernel,
        out_shape=jax.ShapeDtypeStruct((n_seq, half, hw), BF16),
        grid_spec=pltpu.PrefetchScalarGridSpec(
            num_scalar_prefetch=1,
            grid=(n_seq, n_pages // 2),
            in_specs=[pl.BlockSpec((None, rows, hd), lambda b, p, pt: (b, 0, 0)),
                      pl.BlockSpec((None, 2, half, hd), lambda b, p, pt: (b, 0, 0, 0)),
                      pl.BlockSpec((None, half, hw), lambda b, p, pt: (b, 0, 0)),
                      kpage_spec(0), kpage_spec(1), vpage_spec(0), vpage_spec(1),
                      pl.BlockSpec((rows, 1), lambda b, p, pt: (0, 0)),
                      pl.BlockSpec((4, hd), lambda b, p, pt: (0, 0)),
                      pl.BlockSpec((1, hw), lambda b, p, pt: (0, 0))],
            out_specs=pl.BlockSpec((None, half, hw), lambda b, p, pt: (b, 0, 0)),
            scratch_shapes=[pltpu.VMEM((rows, page * n_heads), F32), pltpu.VMEM((rows, LANES), F32),
                            pltpu.VMEM((rows, LANES), F32), pltpu.VMEM((rows, hw), F32)],
        ),
        compiler_params=_params("arbitrary", "arbitrary"),
        name="diff_sample",
    )(page_table, by_component(q.astype(F32)).reshape(n_seq, rows, hd), by_component(k_new),
      v_new.reshape(n_seq, half, hw), cache_k, cache_k, cache_v, cache_v, _row_slopes(n_heads, row_heads), lam_vecs,
      g_sub.reshape(1, hw))
    return out.reshape(t, d)


def _first_index_of_max(vals, idx, n):
    m = jnp.max(vals, axis=0, keepdims=True)
    first = jnp.min(jnp.where(vals == m, idx, float(n)), axis=0, keepdims=True)
    return m, first


def _ffn_prologue_kernel(x_ref, g_ref, sh_ref, sc_ref, wr_ref, br_ref, h_ref, idx_ref, gw_ref):
    h = _modnorm_value(x_ref[...], g_ref[...], sh_ref[...], sc_ref[...])
    h_ref[...] = h
    logits = lax.dot_general(wr_ref[...], h, _NT, precision=lax.Precision.HIGHEST, preferred_element_type=F32)
    s = jax.nn.sigmoid(logits)
    choice = s + br_ref[...]
    n_exp, tm = s.shape
    per = n_exp // N_GROUPS
    sub = lax.broadcasted_iota(I32, (per, tm), 0).astype(F32)
    grp_rows = []
    for g in range(N_GROUPS):
        cg = choice[g * per:(g + 1) * per, :]
        m1, first = _first_index_of_max(cg, sub, per)
        m2 = jnp.max(jnp.where(sub == first, NEG_INF, cg), axis=0, keepdims=True)
        grp_rows.append(m1 + m2)
    grp = jnp.concatenate(grp_rows, axis=0)
    gi = lax.broadcasted_iota(I32, (N_GROUPS, tm), 0).astype(F32)
    keep = jnp.zeros((N_GROUPS, tm), F32)
    for _ in range(TOPK_GROUPS):
        _, first = _first_index_of_max(grp, gi, N_GROUPS)
        hit = gi == first
        keep = jnp.where(hit, 1.0, keep)
        grp = jnp.where(hit, NEG_INF, grp)
    masked = jnp.concatenate(
        [jnp.where(keep[g:g + 1, :] > 0.5, choice[g * per:(g + 1) * per, :], NEG_INF) for g in range(N_GROUPS)],
        axis=0)
    ei = lax.broadcasted_iota(I32, (n_exp, tm), 0).astype(F32)
    idx_rows, w_rows = [], []
    for _ in range(TOP_K):
        _, first = _first_index_of_max(masked, ei, n_exp)
        hit = ei == first
        idx_rows.append(first)
        w_rows.append(jnp.sum(jnp.where(hit, s, 0.0), axis=0, keepdims=True))
        masked = jnp.where(hit, NEG_INF, masked)
    w = jnp.concatenate(w_rows, axis=0)
    idx_ref[...] = jnp.concatenate(idx_rows, axis=0).astype(I32)
    gw_ref[...] = w / jnp.sum(w, axis=0, keepdims=True) * ROUTED_SCALE


def _ffn_prologue(x, g, mod, w_router_t, b_router, tm):
    t, d = x.shape
    n_exp = w_router_t.shape[0]
    return pl.pallas_call(
        _ffn_prologue_kernel,
        out_shape=(jax.ShapeDtypeStruct((t, d), F32), jax.ShapeDtypeStruct((TOP_K, t), I32),
                   jax.ShapeDtypeStruct((TOP_K, t), F32)),
        grid=(t // tm,),
        in_specs=[pl.BlockSpec((tm, d), lambda i: (i, 0)),
                  pl.BlockSpec((1, d), lambda i: (0, 0)),
                  mod.spec(0, tm, d, d, 0),
                  mod.spec(1, tm, d, d, 0),
                  pl.BlockSpec((n_exp, d), lambda i: (0, 0)),
                  pl.BlockSpec((n_exp, 1), lambda i: (0, 0))],
        out_specs=(pl.BlockSpec((tm, d), lambda i: (i, 0)),
                   pl.BlockSpec((TOP_K, tm), lambda i: (0, i)),
                   pl.BlockSpec((TOP_K, tm), lambda i: (0, i))),
        compiler_params=_params("arbitrary"),
        name="ffn_prologue",
    )(x, g.reshape(1, d), mod.arr, mod.arr, w_router_t, b_router.reshape(n_exp, 1))


def _swiglu(x, wg_s, wu_s, wd_s):
    gate = jnp.dot(x, wg_s[...], preferred_element_type=F32)
    up = jnp.dot(x, wu_s[...], preferred_element_type=F32)
    a = (gate * jax.nn.sigmoid(gate) * up).astype(BF16)
    return jnp.dot(a, wd_s[...], preferred_element_type=F32)


def _row_gather_copy(src_hbm, src_row, dst, dst_row, sem):
    return pltpu.make_async_copy(src_hbm.at[pl.ds(src_row, 1), :], dst.at[pl.ds(dst_row, 1), :], sem)


def _expert_kernel(be_ref, nv_ref, tok_ref, h_hbm, wg_ref, wu_ref, wd_ref, y_ref, wg_s, wu_s, wd_s, x_s, sem):
    i = pl.program_id(0)
    bm = y_ref.shape[0]
    n_live = nv_ref[0]
    live = i < n_live
    changed = jnp.logical_or(i == 0, be_ref[i] != be_ref[jnp.maximum(i - 1, 0)])

    def start_gather(blk, slot):
        def body(r, carry):
            _row_gather_copy(h_hbm, tok_ref[blk * bm + r], x_s.at[slot], r, sem.at[slot]).start()
            return carry

        lax.fori_loop(0, bm, body, 0, unroll=8)

    @pl.when(jnp.logical_and(i == 0, live))
    def _():
        start_gather(0, 0)

    @pl.when(i + 1 < n_live)
    def _():
        start_gather(i + 1, (i + 1) % 2)

    @pl.when(jnp.logical_and(live, changed))
    def _():
        wg_s[...] = wg_ref[...].astype(BF16)
        wu_s[...] = wu_ref[...].astype(BF16)
        wd_s[...] = wd_ref[...].astype(BF16)

    @pl.when(live)
    def _():
        slot = i % 2
        pltpu.make_async_copy(h_hbm.at[pl.ds(0, bm), :], x_s.at[slot], sem.at[slot]).wait()
        y_ref[...] = _swiglu(x_s[slot].astype(BF16), wg_s, wu_s, wd_s)

    @pl.when(jnp.logical_not(live))
    def _():
        y_ref[...] = jnp.zeros(y_ref.shape, y_ref.dtype)


def _experts(h, slot_tok, block_e, n_valid, wg, wu, wd, layer, bm):
    t, d = h.shape
    f = wg.shape[-1]
    n_blocks = slot_tok.shape[0] // bm
    up_spec = pl.BlockSpec((None, None, d, f), lambda i, be, nv, tok: (layer, be[i], 0, 0))
    down_spec = pl.BlockSpec((None, None, f, d), lambda i, be, nv, tok: (layer, be[i], 0, 0))
    return pl.pallas_call(
        _expert_kernel,
        out_shape=jax.ShapeDtypeStruct((n_blocks * bm, d), F32),
        grid_spec=pltpu.PrefetchScalarGridSpec(
            num_scalar_prefetch=3,
            grid=(n_blocks,),
            in_specs=[pl.BlockSpec(memory_space=pl.ANY), up_spec, up_spec, down_spec],
            out_specs=pl.BlockSpec((bm, d), lambda i, be, nv, tok: (i, 0)),
            scratch_shapes=[pltpu.VMEM((d, f), BF16), pltpu.VMEM((d, f), BF16), pltpu.VMEM((f, d), BF16),
                            pltpu.VMEM((2, bm, d), F32), pltpu.SemaphoreType.DMA((2,))],
        ),
        compiler_params=_params("arbitrary"),
        name="experts",
    )(block_e, n_valid, slot_tok, h, wg, wu, wd)


def _shared_kernel(x_ref, wg_ref, wu_ref, wd_ref, y_ref, wg_s, wu_s, wd_s):
    @pl.when(pl.program_id(0) == 0)
    def _():
        wg_s[...] = wg_ref[...].astype(BF16)
        wu_s[...] = wu_ref[...].astype(BF16)
        wd_s[...] = wd_ref[...].astype(BF16)

    y_ref[...] = _swiglu(x_ref[...].astype(BF16), wg_s, wu_s, wd_s)


def _shared_expert(h, wg, wu, wd, layer, tm):
    t, d = h.shape
    f = wg.shape[-1]
    return pl.pallas_call(
        _shared_kernel,
        out_shape=jax.ShapeDtypeStruct((t, d), F32),
        grid=(t // tm,),
        in_specs=[pl.BlockSpec((tm, d), lambda i: (i, 0)),
                  pl.BlockSpec((None, d, f), lambda i: (layer, 0, 0)),
                  pl.BlockSpec((None, d, f), lambda i: (layer, 0, 0)),
                  pl.BlockSpec((None, f, d), lambda i: (layer, 0, 0))],
        out_specs=pl.BlockSpec((tm, d), lambda i: (i, 0)),
        scratch_shapes=[pltpu.VMEM((d, f), BF16), pltpu.VMEM((d, f), BF16), pltpu.VMEM((f, d), BF16)],
        compiler_params=_params("arbitrary"),
        name="shared_expert",
    )(h, wg, wu, wd)


def _dispatch_plan(idx_t, n_exp, bm):
    k, t = idx_t.shape
    a = t * k
    onehot = (idx_t.T[:, :, None] == jnp.arange(n_exp, dtype=I32)[None, None, :]).astype(I32)
    per_token = jnp.sum(onehot, axis=1)
    seen = jnp.cumsum(per_token, axis=0)
    before = seen - per_token
    counts = seen[-1]
    padded = (counts + bm - 1) // bm * bm
    pad_end = jnp.cumsum(padded)
    pad_start = pad_end - padded
    pos = jnp.sum(onehot * (before + pad_start[None, :])[:, None, :], axis=2).reshape(a).astype(I32)
    n_blocks = -(-a // bm) + n_exp
    slot_tok = jnp.zeros((n_blocks * bm,), I32).at[pos].set(jnp.arange(a, dtype=I32) // k)
    block_start = jnp.arange(n_blocks, dtype=pad_end.dtype) * bm
    block_e = jnp.minimum(jnp.sum(pad_end[None, :] <= block_start[:, None], axis=1), n_exp - 1).astype(I32)
    n_valid = (pad_end[-1:] // bm).astype(I32)
    return slot_tok, block_e, n_valid, pos


def _ffn_out_kernel(pos_ref, x_ref, gate_ref, gw_ref, shared_ref, yb_hbm, o_ref, y_s, sem, *, row_off, top_k):
    i = pl.program_id(0)
    n = pl.num_programs(0)
    tm = x_ref.shape[0]

    def start_gather(tile, slot):
        base = (row_off + tile) * tm * top_k

        def body(r, carry):
            for k in range(top_k):
                _row_gather_copy(yb_hbm, pos_ref[base + r * top_k + k], y_s.at[slot, k], r, sem.at[slot]).start()
            return carry

        lax.fori_loop(0, tm, body, 0, unroll=2)

    @pl.when(i == 0)
    def _():
        start_gather(0, 0)

    @pl.when(i + 1 < n)
    def _():
        start_gather(i + 1, (i + 1) % 2)

    slot = i % 2
    for k in range(top_k):
        pltpu.make_async_copy(yb_hbm.at[pl.ds(0, tm), :], y_s.at[slot, k], sem.at[slot]).wait()
    gw = gw_ref[...]
    routed = gw[:, 0:1] * y_s[slot, 0]
    for k in range(1, top_k):
        routed = routed + gw[:, k:k + 1] * y_s[slot, k]
    o_ref[...] = x_ref[...] + gate_ref[...] * (routed + shared_ref[...])


def _ffn_out(x, mod, gw, pos, yb, shared, row_off, tm):
    t, d = x.shape
    top_k = gw.shape[1]
    return pl.pallas_call(
        functools.partial(_ffn_out_kernel, row_off=row_off, top_k=top_k),
        out_shape=jax.ShapeDtypeStruct((t, d), F32),
        grid_spec=pltpu.PrefetchScalarGridSpec(
            num_scalar_prefetch=1,
            grid=(t // tm,),
            in_specs=[pl.BlockSpec((tm, d), lambda i, pos: (i, 0)),
                      mod.spec(2, tm, d, d, 0),
                      pl.BlockSpec((tm, top_k), lambda i, pos: (row_off + i, 0)),
                      pl.BlockSpec((tm, d), lambda i, pos: (row_off + i, 0)),
                      pl.BlockSpec(memory_space=pl.ANY)],
            out_specs=pl.BlockSpec((tm, d), lambda i, pos: (i, 0)),
            scratch_shapes=[pltpu.VMEM((2, top_k, tm, d), F32), pltpu.SemaphoreType.DMA((2,))],
        ),
        compiler_params=_params("arbitrary"),
        name="ffn_out",
    )(pos, x, mod.arr, gw, shared, yb)


def kernel(x_prompt, x_sample, cache_k_moba, cache_v_moba, cache_k_diff, cache_v_diff, page_table, c_prompt, c_sample, g_mix, w_ada_mix, b_ada_mix, w_qkv_moba, g_q_moba, g_k_moba, w_o_moba, w_qkv_diff, g_q_diff, g_k_diff, lambda_q1, lambda_k1, lambda_q2, lambda_k2, g_sub_diff, w_o_diff, g_ffn, w_ada_ffn, b_ada_ffn, w_router, b_router, w_gate_exp, w_up_exp, w_down_exp, w_gate_sh, w_up_sh, w_down_sh):
    n_p, seq, d = x_prompt.shape
    n_s, n_new, _ = x_sample.shape
    depth = g_mix.shape[0]
    t_p, t_s = n_p * seq, n_s * n_new
    moba_heads = cache_k_moba.shape[3]
    diff_heads = cache_k_diff.shape[3]
    n_exp = w_router.shape[2]
    tm_p = 512
    bm = 256
    tm_out = 64

    xp = x_prompt.reshape(t_p, d)
    xs = x_sample.reshape(t_s, d)

    c_all = jnp.concatenate([c_prompt, c_sample], axis=0)
    n_c = c_all.shape[0]
    c_all = jnp.pad(c_all, ((0, -n_c % 8), (0, 0)))
    ada_mix = _ada(c_all, w_ada_mix, b_ada_mix)
    ada_ffn = _ada(c_all, w_ada_ffn, b_ada_ffn)

    def mods(ada, i):
        mp = _Mod(ada[i, :n_p].reshape(n_p, 1, 3 * d), seq)
        ms = _Mod(jnp.repeat(ada[i, n_p:n_p + n_s], n_new, axis=0), n_new)
        return mp, ms

    ck_d = cache_k_diff.reshape(cache_k_diff.shape[:3] + (2 * diff_heads, d // (2 * diff_heads)))

    kv = {name: [] for name in ("k_mp", "v_mp", "k_dp", "v_dp", "k_ms", "v_ms", "k_ds", "v_ds")}
    for i in range(depth):
        j = i // 2
        mod_p, mod_s = mods(ada_mix, i)
        hp = _modnorm(xp, g_mix[i], mod_p, tm_p)
        hs = _modnorm(xs, g_mix[i], mod_s, t_s)
        if i % 2 == 0:
            gq = jnp.tile(g_q_moba[j], moba_heads)
            gk = jnp.tile(g_k_moba[j], moba_heads)
            qp, kp, vp = _qkv(hp, w_qkv_moba, j, gq, gk, tm_p)
            qs, ks, vs = _qkv(hs, w_qkv_moba, j, gq, gk, t_s)
            op = _moba_prompt(qp, kp, vp, n_p, moba_heads)
            os_ = _moba_sample(qs, ks, vs, cache_k_moba, cache_v_moba, j, page_table, moba_heads)
            w_o = w_o_moba
            kv["k_mp"].append(kp.reshape(n_p, seq, moba_heads, d // moba_heads))
            kv["v_mp"].append(vp.reshape(n_p, seq, moba_heads, d // moba_heads))
            kv["k_ms"].append(ks.reshape(n_s, n_new, moba_heads, d // moba_heads))
            kv["v_ms"].append(vs.reshape(n_s, n_new, moba_heads, d // moba_heads))
        else:
            lam_init = 0.8 - 0.6 * math.exp(-0.3 * i)
            lam_vecs = jnp.stack([lambda_q1[j], lambda_k1[j], lambda_q2[j], lambda_k2[j]]).astype(F32)
            gq = jnp.tile(g_q_diff[j].reshape(-1), diff_heads)
            gk = jnp.tile(g_k_diff[j].reshape(-1), diff_heads)
            qp, kp, vp = _qkv(hp, w_qkv_diff, j, gq, gk, tm_p)
            qs, ks, vs = _qkv(hs, w_qkv_diff, j, gq, gk, t_s)
            op = _diff_prompt(qp, kp, vp, lam_vecs, g_sub_diff[j], n_p, diff_heads, lam_init)
            os_ = _diff_sample(qs, ks, vs, ck_d, cache_v_diff, j, page_table, diff_heads, lam_vecs,
                               g_sub_diff[j], lam_init)
            w_o = w_o_diff
            hw = d // diff_heads
            kv["k_dp"].append(kp.reshape(n_p, seq, diff_heads, 2, hw // 2))
            kv["v_dp"].append(vp.reshape(n_p, seq, diff_heads, hw))
            kv["k_ds"].append(ks.reshape(n_s, n_new, diff_heads, 2, hw // 2))
            kv["v_ds"].append(vs.reshape(n_s, n_new, diff_heads, hw))
        xp = _oproj(op, w_o, j, xp, mod_p, tm_p)
        xs = _oproj(os_, w_o, j, xs, mod_s, t_s)

        mod_p, mod_s = mods(ada_ffn, i)
        w_rt = w_router[i].T
        hp, idx_p, gw_p = _ffn_prologue(xp, g_ffn[i], mod_p, w_rt, b_router[i], 256)
        hs, idx_s, gw_s = _ffn_prologue(xs, g_ffn[i], mod_s, w_rt, b_router[i], t_s)
        h_all = jnp.concatenate([hp, hs], axis=0)
        idx_t = jnp.concatenate([idx_p, idx_s], axis=1)
        gw = jnp.concatenate([gw_p, gw_s], axis=1).T
        slot_tok, block_e, n_valid, pos = _dispatch_plan(idx_t, n_exp, bm)
        yb = _experts(h_all, slot_tok, block_e, n_valid, w_gate_exp, w_up_exp, w_down_exp, i, bm)
        shared = _shared_expert(h_all, w_gate_sh, w_up_sh, w_down_sh, i, t_s)
        xp = _ffn_out(xp, mod_p, gw, pos, yb, shared, 0, tm_out)
        xs = _ffn_out(xs, mod_s, gw, pos, yb, shared, t_p // tm_out, tm_out)

    return (xp.reshape(n_p, seq, d), xs.reshape(n_s, n_new, d),
            jnp.stack(kv["k_mp"]), jnp.stack(kv["v_mp"]), jnp.stack(kv["k_dp"]), jnp.stack(kv["v_dp"]),
            jnp.stack(kv["k_ms"]), jnp.stack(kv["v_ms"]), jnp.stack(kv["k_ds"]), jnp.stack(kv["v_ds"]))
```
